```python
import math
import jax, jax.numpy as jnp
from jax import lax
import numpy as np

D_MODEL = 2048
BATCH = 16
SEQ = 256
DEPTH = 4
DEC_BATCH = 8
DEC_SEQ = 2048
PAST_LEN = 512

GRID_W = 64
N_MIXERS = 3
D_FF = 4 * D_MODEL
EPS = 1e-6
CHUNK = 128
CONV_K = 4
CONV_PAD = (CONV_K // 2, CONV_K - 1 - CONV_K // 2)

SSD_EXPAND = 2
SSD_D_INNER = SSD_EXPAND * D_MODEL
SSD_HEAD_DIM = 64
SSD_HEADS = SSD_D_INNER // SSD_HEAD_DIM
SSD_GROUPS = 8
SSD_STATE = 128
SSD_CONV_CH = SSD_D_INNER + 2 * SSD_GROUPS * SSD_STATE
SSD_IN = SSD_D_INNER + SSD_CONV_CH + 2 * SSD_HEADS

ML_HEADS = 8
ML_DV = D_MODEL // ML_HEADS
ML_DQK = ML_DV // 2
ML_IN = 2 * ML_HEADS * ML_DQK + 2 * D_MODEL + 4 * ML_HEADS

LRU_WIDTH = D_MODEL
LRU_BLOCKS = 8
LRU_BS = LRU_WIDTH // LRU_BLOCKS
LRU_C = 8.0

N_SSD = len(range(0, DEPTH, N_MIXERS))
N_ML = len(range(1, DEPTH, N_MIXERS))
N_LRU = len(range(2, DEPTH, N_MIXERS))

kernel_name = "hybrid_ssd_mlstm_rglru_diffusion_step"

F32 = jnp.float32


def _rmsnorm(x, w):
    xf = x.astype(F32)
    y = xf * lax.rsqrt(jnp.mean(xf * xf, axis=-1, keepdims=True) + EPS)
    return (y * w.astype(F32)).astype(x.dtype)


def _flip(t):
    return jnp.flip(t, axis=1)


def _dwconv(x, w, b):
    out = lax.conv_general_dilated(
        x, w[:, None, :].astype(x.dtype), window_strides=(1,), padding=[CONV_PAD],
        dimension_numbers=("NWC", "WIO", "NWC"), feature_group_count=x.shape[-1])
    return out + b.astype(x.dtype)


def _to_col_major(h):
    b, l, d = h.shape
    rows = l // GRID_W
    return h.reshape(b, rows, GRID_W, d).transpose(0, 2, 1, 3).reshape(b, l, d)


def _to_row_major(h):
    b, l, d = h.shape
    rows = l // GRID_W
    return h.reshape(b, GRID_W, rows, d).transpose(0, 2, 1, 3).reshape(b, l, d)


def _segsum(a):
    t = a.shape[-1]
    cs = jnp.cumsum(a, axis=-1)
    d = cs[..., :, None] - cs[..., None, :]
    return jnp.where(jnp.tril(jnp.ones((t, t), bool)), d, -jnp.inf)


def _ssd_scan(x, dt, A, bm, cm, h0):
    b, l, h, p = x.shape
    g, n = bm.shape[-2:]
    r = h // g
    c = l // CHUNK
    xdt = (x * dt[..., None]).reshape(b, c, CHUNK, g, r, p)
    a = (dt * A).reshape(b, c, CHUNK, g, r).transpose(0, 3, 4, 1, 2)
    bc = bm.reshape(b, c, CHUNK, g, n)
    cc = cm.reshape(b, c, CHUNK, g, n)
    a_cs = jnp.cumsum(a, axis=-1)
    lmat = jnp.exp(_segsum(a))
    cb = jnp.einsum("bclgn,bcsgn->bgcls", cc, bc)
    scores = cb[:, :, None] * lmat
    y_diag = jnp.einsum("bgrcls,bcsgrp->bclgrp", scores, xdt)
    decay_states = jnp.exp(a_cs[..., -1:] - a_cs)
    states = jnp.einsum("bclgn,bgrcl,bclgrp->bcgrpn", bc, decay_states, xdt)
    states = jnp.concatenate([h0.reshape(b, g, r, p, n)[:, None], states], axis=1)
    chunk_a = jnp.pad(a_cs[..., -1], ((0, 0), (0, 0), (0, 0), (1, 0)))
    decay_chunk = jnp.exp(_segsum(chunk_a))
    new_states = jnp.einsum("bgrzc,bcgrpn->bzgrpn", decay_chunk, states)
    states_in, final = new_states[:, :-1], new_states[:, -1]
    y_off = jnp.einsum("bclgn,bcgrpn,bgrcl->bclgrp", cc, states_in, jnp.exp(a_cs))
    y = (y_diag + y_off).reshape(b, l, h, p)
    return y, final.reshape(b, h, p, n)


def _ssd_mixer(u, h0, P, j):
    b, l, _ = u.shape
    proj = u @ P["ssd_in_w"][j]
    z, xbc, dt = jnp.split(proj, [SSD_D_INNER, SSD_D_INNER + SSD_CONV_CH], axis=-1)
    xbc = jax.nn.silu(_dwconv(xbc, P["ssd_conv_w"][j], P["ssd_conv_b"][j])).astype(F32)
    xs, bm, cm = jnp.split(xbc, [SSD_D_INNER, SSD_D_INNER + SSD_GROUPS * SSD_STATE], axis=-1)
    xs = xs.reshape(b, l, SSD_HEADS, SSD_HEAD_DIM)
    bm = bm.reshape(b, l, SSD_GROUPS, SSD_STATE)
    cm = cm.reshape(b, l, SSD_GROUPS, SSD_STATE)
    dt = jax.nn.softplus(dt.astype(F32).reshape(b, l, 2, SSD_HEADS) + P["ssd_dt_bias"][j].astype(F32))
    A = -jnp.exp(P["ssd_a_log"][j].astype(F32))
    h0 = h0.astype(F32)
    y_f, s_f = _ssd_scan(xs, dt[:, :, 0], A[0], bm, cm, h0[:, 0])
    y_b, s_b = _ssd_scan(_flip(xs), _flip(dt[:, :, 1]), A[1], _flip(bm), _flip(cm), h0[:, 1])
    y = y_f + _flip(y_b) + P["ssd_d"][j].astype(F32)[:, None] * xs
    y = y.reshape(b, l, SSD_D_INNER) * jax.nn.silu(z.astype(F32))
    y = _rmsnorm(y.reshape(b, l, SSD_GROUPS, -1),
                 P["ssd_norm_w"][j].reshape(SSD_GROUPS, -1)).reshape(b, l, SSD_D_INNER)
    return y.astype(u.dtype) @ P["ssd_out_w"][j], jnp.stack([s_f, s_b], axis=1)


def _mlstm_scan(q, k, v, ig, lf, C0, n0, m0):
    b, l, h, _ = q.shape
    c = l // CHUNK
    causal = jnp.tril(jnp.ones((CHUNK, CHUNK), bool))

    def to_chunks(t):
        return t.reshape(b, c, CHUNK, *t.shape[2:]).swapaxes(0, 1)

    def step(carry, inp):
        C, n, m = carry
        qc, kc, vc, igc, lfc = inp
        bcs = jnp.cumsum(lfc, axis=1).swapaxes(1, 2)
        igt = igc.swapaxes(1, 2)
        dmat = jnp.where(causal, bcs[..., :, None] - bcs[..., None, :] + igt[..., None, :], -jnp.inf)
        inter = bcs + m[..., None]
        mt = jnp.maximum(inter, jnp.max(dmat, axis=-1))
        s = jnp.einsum("bthd,bshd->bhts", qc, kc) * jnp.exp(dmat - mt[..., None])
        w_in = jnp.exp(inter - mt)
        num = (jnp.einsum("bhts,bshv->bthv", s, vc)
               + jnp.einsum("bthd,bhdv->bthv", qc, C) * w_in.swapaxes(1, 2)[..., None])
        den = jnp.sum(s, axis=-1) + jnp.einsum("bthd,bhd->bht", qc, n) * w_in
        hout = num / jnp.maximum(jnp.abs(den), jnp.exp(-mt)).swapaxes(1, 2)[..., None]
        gl = bcs[..., -1:] - bcs + igt
        m_new = jnp.maximum(bcs[..., -1] + m, jnp.max(gl, axis=-1))
        ws = jnp.exp(gl - m_new[..., None])
        dec = jnp.exp(bcs[..., -1] + m - m_new)
        C_new = dec[..., None, None] * C + jnp.einsum("bhs,bshd,bshv->bhdv", ws, kc, vc)
        n_new = dec[..., None] * n + jnp.einsum("bhs,bshd->bhd", ws, kc)
        return (C_new, n_new, m_new), hout

    (C, n, m), hs = lax.scan(step, (C0, n0, m0), tuple(to_chunks(t) for t in (q, k, v, ig, lf)))
    return hs.swapaxes(0, 1).reshape(b, l, h, -1), C, n, m


def _mlstm_mixer(u, C0, n0, m0, P, j):
    b, l, _ = u.shape
    proj = u @ P["ml_in_w"][j]
    hk = ML_HEADS * ML_DQK
    q, k, v, o, gates = jnp.split(proj, [hk, 2 * hk, 2 * hk + D_MODEL, 2 * hk + 2 * D_MODEL], axis=-1)
    q = q.astype(F32).reshape(b, l, ML_HEADS, ML_DQK)
    k = k.astype(F32).reshape(b, l, ML_HEADS, ML_DQK) * (ML_DQK ** -0.5)
    v = v.astype(F32).reshape(b, l, ML_HEADS, ML_DV)
    gates = gates.astype(F32).reshape(b, l, 2, 2, ML_HEADS)
    ig = gates[:, :, :, 0] + P["ml_igate_b"][j].astype(F32)
    lf = jax.nn.log_sigmoid(gates[:, :, :, 1] + P["ml_fgate_b"][j].astype(F32))
    C0, n0, m0 = C0.astype(F32), n0.astype(F32), m0.astype(F32)
    h_f, Cf, nf, mf = _mlstm_scan(q, k, v, ig[:, :, 0], lf[:, :, 0], C0[:, 0], n0[:, 0], m0[:, 0])
    h_b, Cb, nb, mb = _mlstm_scan(_flip(q), _flip(k), _flip(v), _flip(ig[:, :, 1]), _flip(lf[:, :, 1]),
                                  C0[:, 1], n0[:, 1], m0[:, 1])
    hsum = _rmsnorm(h_f + _flip(h_b), P["ml_norm_w"][j].reshape(ML_HEADS, ML_DV))
    y = hsum.reshape(b, l, D_MODEL) * jax.nn.sigmoid(o.astype(F32))
    return (y.astype(u.dtype) @ P["ml_out_w"][j], jnp.stack([Cf, Cb], axis=1),
            jnp.stack([nf, nb], axis=1), jnp.stack([mf, mb], axis=1))


def _rglru_scan(x, wa, ba, wi, bi, lam, h0):
    b, l, w = x.shape
    xb = x.reshape(b, l, LRU_BLOCKS, LRU_BS)
    r = jax.nn.sigmoid(jnp.einsum("blnk,nkj->blnj", xb, wa.astype(F32)).reshape(b, l, w) + ba.astype(F32))
    i = jax.nn.sigmoid(jnp.einsum("blnk,nkj->blnj", xb, wi.astype(F32)).reshape(b, l, w) + bi.astype(F32))
    log_a = -LRU_C * r * jax.nn.softplus(-lam.astype(F32))
    a = jnp.exp(log_a)
    bterm = jnp.sqrt(-jnp.expm1(2.0 * log_a)) * (i * x)
    bterm = bterm.at[:, 0].add(a[:, 0] * h0)

    def combine(e1, e2):
        a1, b1 = e1
        a2, b2 = e2
        return a1 * a2, a2 * b1 + b2

    _, hseq = lax.associative_scan(combine, (a, bterm), axis=1)
    return hseq, hseq[:, -1]


def _rglru_mixer(u, h0, P, j):
    proj = u @ P["lru_in_w"][j]
    gate, xr = jnp.split(proj, [LRU_WIDTH], axis=-1)
    xr = _dwconv(xr, P["lru_conv_w"][j], P["lru_conv_b"][j]).astype(F32)
    h0 = h0.astype(F32)
    wa, ba, wi, bi, lam = (P["lru_wa"][j], P["lru_ba"][j], P["lru_wi"][j], P["lru_bi"][j], P["lru_lambda"][j])
    h_f, s_f = _rglru_scan(xr, wa[0], ba[0], wi[0], bi[0], lam[0], h0[:, 0])
    h_b, s_b = _rglru_scan(_flip(xr), wa[1], ba[1], wi[1], bi[1], lam[1], h0[:, 1])
    y = (h_f + _flip(h_b)) * jax.nn.gelu(gate.astype(F32))
    return y.astype(u.dtype) @ P["lru_out_w"][j], jnp.stack([s_f, s_b], axis=1)


def _trunk(x, cond, grid, ssd_h0, ml_C0, ml_n0, ml_m0, lru_h0, P):
    ssd_s, ml_c, ml_n, ml_m, lru_s = [], [], [], [], []
    for l in range(DEPTH):
        mod = (jax.nn.silu(cond.astype(F32)) @ P["mod_w"][l].astype(F32)
               + P["mod_b"][l].astype(F32)).astype(x.dtype)[:, None, :]
        sh1, sc1, g1, sh2, sc2, g2 = jnp.split(mod, 6, axis=-1)
        h = _rmsnorm(x, P["norm_mix_pre"][l]) * (1 + sc1) + sh1
        col = grid and (l // N_MIXERS) % 2 == 1
        if col:
            h = _to_col_major(h)
        kind, j = l % N_MIXERS, l // N_MIXERS
        if kind == 0:
            y, s = _ssd_mixer(h, ssd_h0[:, j], P, j)
            ssd_s.append(s)
        elif kind == 1:
            y, sc, sn, sm = _mlstm_mixer(h, ml_C0[:, j], ml_n0[:, j], ml_m0[:, j], P, j)
            ml_c.append(sc)
            ml_n.append(sn)
            ml_m.append(sm)
        else:
            y, s = _rglru_mixer(h, lru_h0[:, j], P, j)
            lru_s.append(s)
        if col:
            y = _to_row_major(y)
        x = x + g1 * _rmsnorm(y, P["norm_mix_post"][l])
        hm = _rmsnorm(x, P["norm_mlp_pre"][l]) * (1 + sc2) + sh2
        f = jnp.square(jax.nn.relu(hm @ P["mlp_w1"][l])) @ P["mlp_w2"][l]
        x = x + g2 * _rmsnorm(f, P["norm_mlp_post"][l])
    return (x, jnp.stack(ssd_s, axis=1), jnp.stack(ml_c, axis=1), jnp.stack(ml_n, axis=1),
            jnp.stack(ml_m, axis=1), jnp.stack(lru_s, axis=1))


def setup_inputs(seed: int = 0) -> dict:
    key = jax.random.key(seed)
    ks = iter(jax.random.split(key, 64))

    def nrm(shape, scale):
        return scale * jax.random.normal(next(ks), shape, F32)

    def gain(shape):
        return 1.0 + nrm(shape, 0.02)

    def unif(shape, lo, hi):
        return jax.random.uniform(next(ks), shape, F32, minval=lo, maxval=hi)

    d = D_MODEL
    inp = {}
    inp["x_prompt"] = nrm((BATCH, SEQ, d), 1.0)
    inp["x_sample"] = nrm((DEC_BATCH, DEC_SEQ, d), 1.0)
    inp["state_ssd"] = nrm((DEC_BATCH, N_SSD, 2, SSD_HEADS, SSD_HEAD_DIM, SSD_STATE), 0.1)
    inp["state_mlstm_C"] = nrm((DEC_BATCH, N_ML, 2, ML_HEADS, ML_DQK, ML_DV), 0.1)
    inp["state_mlstm_n"] = nrm((DEC_BATCH, N_ML, 2, ML_HEADS, ML_DQK), 0.1)
    inp["state_mlstm_m"] = nrm((DEC_BATCH, N_ML, 2, ML_HEADS), 1.0)
    inp["state_rglru"] = nrm((DEC_BATCH, N_LRU, 2, LRU_WIDTH), 0.5)
    inp["c"] = nrm((DEC_BATCH, d), 1.0)
    inp["c_ctx"] = nrm((d,), 1.0)
    inp["mod_w"] = nrm((DEPTH, d, 6 * d), 0.5 * d ** -0.5)
    inp["mod_b"] = nrm((DEPTH, 6 * d), 0.02)
    inp["norm_mix_pre"] = gain((DEPTH, d))
    inp["norm_mix_post"] = gain((DEPTH, d))
    inp["norm_mlp_pre"] = gain((DEPTH, d))
    inp["norm_mlp_post"] = gain((DEPTH, d))
    inp["mlp_w1"] = nrm((DEPTH, d, D_FF), d ** -0.5)
    inp["mlp_w2"] = nrm((DEPTH, D_FF, d), D_FF ** -0.5)
    inp["ssd_in_w"] = nrm((N_SSD, d, SSD_IN), d ** -0.5)
    inp["ssd_conv_w"] = nrm((N_SSD, CONV_K, SSD_CONV_CH), CONV_K ** -0.5)
    inp["ssd_conv_b"] = nrm((N_SSD, SSD_CONV_CH), 0.02)
    dt0 = jnp.exp(unif((N_SSD, 2, SSD_HEADS), math.log(1e-3), math.log(1e-1)))
    inp["ssd_dt_bias"] = dt0 + jnp.log(-jnp.expm1(-dt0))
    inp["ssd_a_log"] = jnp.log(unif((N_SSD, 2, SSD_HEADS), 1.0, 16.0))
    inp["ssd_d"] = gain((N_SSD, SSD_HEADS))
    inp["ssd_norm_w"] = gain((N_SSD, SSD_D_INNER))
    inp["ssd_out_w"] = nrm((N_SSD, SSD_D_INNER, d), SSD_D_INNER ** -0.5)
    inp["ml_in_w"] = nrm((N_ML, d, ML_IN), d ** -0.5)
    inp["ml_igate_b"] = nrm((N_ML, 2, ML_HEADS), 0.1)
    inp["ml_fgate_b"] = unif((N_ML, 2, ML_HEADS), 3.0, 6.0)
    inp["ml_norm_w"] = gain((N_ML, d))
    inp["ml_out_w"] = nrm((N_ML, d, d), d ** -0.5)
    inp["lru_in_w"] = nrm((N_LRU, d, 2 * LRU_WIDTH), d ** -0.5)
    inp["lru_conv_w"] = nrm((N_LRU, CONV_K, LRU_WIDTH), CONV_K ** -0.5)
    inp["lru_conv_b"] = nrm((N_LRU, LRU_WIDTH), 0.02)
    inp["lru_wa"] = nrm((N_LRU, 2, LRU_BLOCKS, LRU_BS, LRU_BS), LRU_BS ** -0.5)
    inp["lru_ba"] = nrm((N_LRU, 2, LRU_WIDTH), 0.02)
    inp["lru_wi"] = nrm((N_LRU, 2, LRU_BLOCKS, LRU_BS, LRU_BS), LRU_BS ** -0.5)
    inp["lru_bi"] = nrm((N_LRU, 2, LRU_WIDTH), 0.02)
    a0 = unif((N_LRU, 2, LRU_WIDTH), 0.9, 0.999) ** (1.0 / LRU_C)
    inp["lru_lambda"] = jnp.log(a0) - jnp.log1p(-a0)
    inp["lru_out_w"] = nrm((N_LRU, LRU_WIDTH, d), LRU_WIDTH ** -0.5)
    return inp


def reference(x_prompt, x_sample, state_ssd, state_mlstm_C, state_mlstm_n, state_mlstm_m, state_rglru,
              c, c_ctx, mod_w, mod_b, norm_mix_pre, norm_mix_post, norm_mlp_pre, norm_mlp_post,
              mlp_w1, mlp_w2, ssd_in_w, ssd_conv_w, ssd_conv_b, ssd_dt_bias, ssd_a_log, ssd_d,
              ssd_norm_w, ssd_out_w, ml_in_w, ml_igate_b, ml_fgate_b, ml_norm_w, ml_out_w,
              lru_in_w, lru_conv_w, lru_conv_b, lru_wa, lru_ba, lru_wi, lru_bi, lru_lambda, lru_out_w):
    P = dict(mod_w=mod_w, mod_b=mod_b, norm_mix_pre=norm_mix_pre, norm_mix_post=norm_mix_post,
             norm_mlp_pre=norm_mlp_pre, norm_mlp_post=norm_mlp_post, mlp_w1=mlp_w1, mlp_w2=mlp_w2,
             ssd_in_w=ssd_in_w, ssd_conv_w=ssd_conv_w, ssd_conv_b=ssd_conv_b, ssd_dt_bias=ssd_dt_bias,
             ssd_a_log=ssd_a_log, ssd_d=ssd_d, ssd_norm_w=ssd_norm_w, ssd_out_w=ssd_out_w,
             ml_in_w=ml_in_w, ml_igate_b=ml_igate_b, ml_fgate_b=ml_fgate_b, ml_norm_w=ml_norm_w,
             ml_out_w=ml_out_w, lru_in_w=lru_in_w, lru_conv_w=lru_conv_w, lru_conv_b=lru_conv_b,
             lru_wa=lru_wa, lru_ba=lru_ba, lru_wi=lru_wi, lru_bi=lru_bi, lru_lambda=lru_lambda,
             lru_out_w=lru_out_w)
    bp = x_prompt.shape[0]
    cond_ctx = jnp.broadcast_to(c_ctx, (bp, D_MODEL))
    z_ssd = jnp.zeros((bp, N_SSD, 2, SSD_HEADS, SSD_HEAD_DIM, SSD_STATE), F32)
    z_c = jnp.zeros((bp, N_ML, 2, ML_HEADS, ML_DQK, ML_DV), F32)
    z_n = jnp.zeros((bp, N_ML, 2, ML_HEADS, ML_DQK), F32)
    z_m = jnp.zeros((bp, N_ML, 2, ML_HEADS), F32)
    z_lru = jnp.zeros((bp, N_LRU, 2, LRU_WIDTH), F32)
    y_prompt, n_ssd, n_c, n_n, n_m, n_lru = _trunk(x_prompt, cond_ctx, False, z_ssd, z_c, z_n, z_m, z_lru, P)
    y_sample = _trunk(x_sample, c, True, state_ssd, state_mlstm_C, state_mlstm_n, state_mlstm_m,
                      state_rglru, P)[0]
    dt = x_prompt.dtype
    return (y_prompt, y_sample, n_ssd.astype(dt), n_c.astype(dt), n_n.astype(dt), n_m.astype(dt),
            n_lru.astype(dt))
```

```python
import functools

import jax
import jax.numpy as jnp
from jax import lax
from jax.experimental import pallas as pl
from jax.experimental.pallas import tpu as pltpu

F32 = jnp.float32
BF16 = jnp.bfloat16
HIGHEST = lax.Precision.HIGHEST

EPS = 1e-6
CHUNK = 128
GRID_W = 64
N_MIXERS = 3
LRU_C = 8.0
SUBLANES = 8
LANES = 128
VMEM_LIMIT_BYTES = 56 * 1024 * 1024


def _cparams(*semantics):
    return pltpu.CompilerParams(dimension_semantics=semantics, vmem_limit_bytes=VMEM_LIMIT_BYTES)


def _dot(a, b):
    return jnp.dot(a.astype(BF16), b.astype(BF16), preferred_element_type=F32)


def _dot_exact(a, b):
    return jnp.dot(a, b, precision=HIGHEST, preferred_element_type=F32)


def _softplus(x):
    return jnp.maximum(x, 0.0) + jnp.log1p(jnp.exp(-jnp.abs(x)))


def _sigmoid(x):
    return jax.nn.sigmoid(x)


def _silu(x):
    return x * jax.nn.sigmoid(x)


def _rms(x, w):
    return x * lax.rsqrt(jnp.mean(x * x, axis=-1, keepdims=True) + EPS) * w


def _tri(n, lower):
    r = lax.broadcasted_iota(jnp.int32, (n, n), 0)
    c = lax.broadcasted_iota(jnp.int32, (n, n), 1)
    return (r >= c if lower else r <= c).astype(F32)


def _conv_rows(load, c, n_chunks, rows, w, b):
    t0 = pl.multiple_of(c * rows, rows)
    last = (n_chunks - 1) * rows
    main = load(t0, rows)
    prev = load(pl.multiple_of(jnp.maximum(t0 - SUBLANES, 0), SUBLANES), SUBLANES)
    prev = jnp.where(c > 0, prev, 0.0)
    nxt = load(pl.multiple_of(jnp.minimum(t0 + rows, last + rows - SUBLANES), SUBLANES), SUBLANES)
    nxt = jnp.where(c < n_chunks - 1, nxt, 0.0)
    row = lax.broadcasted_iota(jnp.int32, main.shape, 0)
    m1 = jnp.where(row == 0, prev[7:8], pltpu.roll(main, 1, 0))
    m2 = jnp.where(row == 0, prev[6:7], jnp.where(row == 1, prev[7:8], pltpu.roll(main, 2, 0)))
    p1 = jnp.where(row == rows - 1, nxt[0:1], pltpu.roll(main, rows - 1, 0))
    return w[0:1] * m2 + w[1:2] * m1 + w[2:3] * main + w[3:4] * p1 + b


def _expand_heads(col, base):
    t = col.shape[0]
    lane = lax.broadcasted_iota(jnp.int32, (t, LANES), 1)
    parts = []
    for p in range(4):
        lo = col[:, base + 2 * p:base + 2 * p + 1]
        hi = col[:, base + 2 * p + 1:base + 2 * p + 2]
        parts.append(jnp.where(lane < 64, lo, hi))
    return jnp.concatenate(parts, axis=1)


def _mod_kernel(c_ref, w_ref, b_ref, o_ref):
    a = _silu(c_ref[...])
    o_ref[0] = _dot(a, w_ref[0]) + b_ref[0]


def _mod_all(cond, mod_w, mod_b):
    depth, d, n = mod_w.shape
    r = cond.shape[0]
    tn = 1024
    return pl.pallas_call(
        _mod_kernel,
        grid=(depth, n // tn),
        in_specs=[
            pl.BlockSpec((r, d), lambda l, j: (0, 0)),
            pl.BlockSpec((1, d, tn), lambda l, j: (l, 0, j)),
            pl.BlockSpec((1, 1, tn), lambda l, j: (l, 0, j)),
        ],
        out_specs=pl.BlockSpec((1, r, tn), lambda l, j: (l, 0, j)),
        out_shape=jax.ShapeDtypeStruct((depth, r, n), F32),
        compiler_params=_cparams("parallel", "parallel"),
        name="mod_map",
    )(cond, mod_w, mod_b.reshape(depth, 1, n))


def _in_proj_kernel(*refs, has_small):
    if has_small:
        x_ref, nw_ref, sc_ref, sh_ref, w_ref, ws_ref, o_ref, os_ref, h_ref = refs
    else:
        x_ref, nw_ref, sc_ref, sh_ref, w_ref, o_ref, h_ref = refs

    @pl.when(pl.program_id(1) == 0)
    def _():
        h = _rms(x_ref[...], nw_ref[...]) * (1.0 + sc_ref[0]) + sh_ref[0]
        h_ref[...] = h.astype(BF16)
        if has_small:
            os_ref[...] = jnp.dot(h_ref[...], ws_ref[...], preferred_element_type=F32)

    o_ref[...] = jnp.dot(h_ref[...], w_ref[...], preferred_element_type=F32)


def _in_proj(x, norm_w, scale, shift, w_main, w_small=None, *, tm=1024, tn=512):
    b, l, d = x.shape
    m = b * l
    n = w_main.shape[1]
    per_batch = scale.shape[0] > 1
    assert m % tm == 0 and n % tn == 0 and (not per_batch or l % tm == 0)
    mod_idx = (lambda i, j: ((i * tm) // l, 0, 0)) if per_batch else (lambda i, j: (0, 0, 0))
    has_small = w_small is not None
    in_specs = [
        pl.BlockSpec((tm, d), lambda i, j: (i, 0)),
        pl.BlockSpec((1, d), lambda i, j: (0, 0)),
        pl.BlockSpec((1, 1, d), mod_idx),
        pl.BlockSpec((1, 1, d), mod_idx),
        pl.BlockSpec((d, tn), lambda i, j: (0, j)),
    ]
    out_specs = [pl.BlockSpec((tm, tn), lambda i, j: (i, j))]
    out_shape = [jax.ShapeDtypeStruct((m, n), F32)]
    args = [x.reshape(m, d), norm_w.reshape(1, d), scale, shift, w_main]
    if has_small:
        ns = w_small.shape[1]
        in_specs.append(pl.BlockSpec((d, ns), lambda i, j: (0, 0)))
        out_specs.append(pl.BlockSpec((tm, ns), lambda i, j: (i, 0)))
        out_shape.append(jax.ShapeDtypeStruct((m, ns), F32))
        args.append(w_small)
    outs = pl.pallas_call(
        functools.partial(_in_proj_kernel, has_small=has_small),
        grid=(m // tm, n // tn),
        in_specs=in_specs,
        out_specs=out_specs,
        out_shape=out_shape,
        scratch_shapes=[pltpu.VMEM((tm, d), BF16)],
        compiler_params=_cparams("parallel", "arbitrary"),
        name="in_proj",
    )(*args)
    return (outs[0], outs[1]) if has_small else (outs[0], None)


def _out_proj_kernel(y_ref, w_ref, x_ref, g_ref, nw_ref, o_ref, acc_ref, *, nk):
    k = pl.program_id(1)

    @pl.when(k == 0)
    def _():
        acc_ref[...] = jnp.zeros_like(acc_ref)

    acc_ref[...] += jnp.dot(y_ref[...], w_ref[...], preferred_element_type=F32)

    @pl.when(k == nk - 1)
    def _():
        o_ref[...] = x_ref[...] + g_ref[0] * _rms(acc_ref[...], nw_ref[...])


def _out_proj(y, w, x, gate, norm_w, *, tm=512, tk=512):
    b, l, d = x.shape
    m = b * l
    kin = w.shape[0]
    per_batch = gate.shape[0] > 1
    assert m % tm == 0 and kin % tk == 0 and (not per_batch or l % tm == 0)
    mod_idx = (lambda i, k: ((i * tm) // l, 0, 0)) if per_batch else (lambda i, k: (0, 0, 0))
    nk = kin // tk
    out = pl.pallas_call(
        functools.partial(_out_proj_kernel, nk=nk),
        grid=(m // tm, nk),
        in_specs=[
            pl.BlockSpec((tm, tk), lambda i, k: (i, k)),
            pl.BlockSpec((tk, d), lambda i, k: (k, 0)),
            pl.BlockSpec((tm, d), lambda i, k: (i, 0)),
            pl.BlockSpec((1, 1, d), mod_idx),
            pl.BlockSpec((1, d), lambda i, k: (0, 0)),
        ],
        out_specs=pl.BlockSpec((tm, d), lambda i, k: (i, 0)),
        out_shape=jax.ShapeDtypeStruct((m, d), F32),
        scratch_shapes=[pltpu.VMEM((tm, d), F32)],
        compiler_params=_cparams("parallel", "arbitrary"),
        name="out_proj",
    )(y.reshape(m, kin), w, x.reshape(m, d), gate, norm_w.reshape(1, d))
    return out.reshape(b, l, d)


def _mlp_kernel(x_ref, nw_ref, sc_ref, sh_ref, w1_ref, w2_ref, g_ref, pw_ref, o_ref, h_ref, acc_ref, *, nf):
    j = pl.program_id(1)

    @pl.when(j == 0)
    def _():
        h = _rms(x_ref[...], nw_ref[...]) * (1.0 + sc_ref[0]) + sh_ref[0]
        h_ref[...] = h.astype(BF16)
        acc_ref[...] = jnp.zeros_like(acc_ref)

    t = jnp.dot(h_ref[...], w1_ref[...], preferred_element_type=F32)
    t = jnp.square(jnp.maximum(t, 0.0))
    acc_ref[...] += jnp.dot(t.astype(BF16), w2_ref[...], preferred_element_type=F32)

    @pl.when(j == nf - 1)
    def _():
        o_ref[...] = x_ref[...] + g_ref[0] * _rms(acc_ref[...], pw_ref[...])


def _mlp(x, pre_w, scale, shift, w1, w2, gate, post_w, *, tm=512, tf=512):
    b, l, d = x.shape
    m = b * l
    dff = w1.shape[1]
    per_batch = scale.shape[0] > 1
    assert m % tm == 0 and dff % tf == 0 and (not per_batch or l % tm == 0)
    mod_idx = (lambda i, j: ((i * tm) // l, 0, 0)) if per_batch else (lambda i, j: (0, 0, 0))
    nf = dff // tf
    out = pl.pallas_call(
        functools.partial(_mlp_kernel, nf=nf),
        grid=(m // tm, nf),
        in_specs=[
            pl.BlockSpec((tm, d), lambda i, j: (i, 0)),
            pl.BlockSpec((1, d), lambda i, j: (0, 0)),
            pl.BlockSpec((1, 1, d), mod_idx),
            pl.BlockSpec((1, 1, d), mod_idx),
            pl.BlockSpec((d, tf), lambda i, j: (0, j)),
            pl.BlockSpec((tf, d), lambda i, j: (j, 0)),
            pl.BlockSpec((1, 1, d), mod_idx),
            pl.BlockSpec((1, d), lambda i, j: (0, 0)),
        ],
        out_specs=pl.BlockSpec((tm, d), lambda i, j: (i, 0)),
        out_shape=jax.ShapeDtypeStruct((m, d), F32),
        scratch_shapes=[pltpu.VMEM((tm, d), BF16), pltpu.VMEM((tm, d), F32)],
        compiler_params=_cparams("parallel", "arbitrary"),
        name="mlp",
    )(x.reshape(m, d), pre_w.reshape(1, d), scale, shift, w1, w2, gate, post_w.reshape(1, d))
    return out.reshape(b, l, d)


def _ssd_dt_cols(dt_ref, bias_ref, alog_ref, t0):
    dt = _softplus(dt_ref[0, 0, pl.ds(t0, CHUNK), :] + bias_ref[0])
    a = dt * (-jnp.exp(alog_ref[0]))
    cs = _dot_exact(_tri(CHUNK, True), a)
    tot = cs[CHUNK - 1:CHUNK]
    lane = lax.broadcasted_iota(jnp.int32, cs.shape, 1)
    pos = jnp.where(lane < 8, cs, tot - cs + a)
    return pos, dt, tot


def _ssd_dt_rows(dtt_ref, biast_ref, alogt_ref, t0):
    dt = _softplus(dtt_ref[0, 0, :, pl.ds(t0, CHUNK)] + biast_ref[0])
    a = dt * (-jnp.exp(alogt_ref[0]))
    cs = _dot_exact(a, _tri(CHUNK, False))
    tot = cs[:, CHUNK - 1:CHUNK]
    row = lax.broadcasted_iota(jnp.int32, cs.shape, 0)
    pos = jnp.where(row < 8, cs, tot - cs + a)
    return pos, dt


def _ssd_kernel(*refs, seq, has_h0, emit_state):
    (z_ref, x_ref, b_ref, c_ref, cwx_ref, cwb_ref, cwc_ref, cbx_ref, cbb_ref, cbc_ref,
     dt_ref, dtt_ref, bias_ref, biast_ref, alog_ref, alogt_ref, dskip_ref, nw_ref) = refs[:18]
    rest = list(refs[18:])
    h0_ref = rest.pop(0) if has_h0 else None
    y_ref = rest.pop(0)
    st_ref = rest.pop(0) if emit_state else None
    xs_ref, bt_ref, cs_ref, ya_ref, ht_ref = rest
    n_chunks = seq // CHUNK
    hp = xs_ref.shape[1]
    li = lax.broadcasted_iota(jnp.int32, (CHUNK, CHUNK), 0)
    si = lax.broadcasted_iota(jnp.int32, (CHUNK, CHUNK), 1)
    lane = lax.broadcasted_iota(jnp.int32, (CHUNK, LANES), 1)

    def load_state(d):
        if has_h0:
            ht_ref[...] = h0_ref[0, d].reshape(hp, -1).T
        else:
            ht_ref[...] = jnp.zeros_like(ht_ref)

    def store_state(d):
        if emit_state:
            st_ref[0, d] = ht_ref[...].T.reshape(st_ref.shape[2:])

    load_state(0)

    def fwd(c, carry):
        t0 = pl.multiple_of(c * CHUNK, CHUNK)
        rows = pl.ds(t0, CHUNK)
        x = _silu(_conv_rows(lambda s, n: x_ref[0, pl.ds(s, n), :], c, n_chunks, CHUNK, cwx_ref[...], cbx_ref[...]))
        bm = _silu(_conv_rows(lambda s, n: b_ref[0, pl.ds(s, n), :], c, n_chunks, CHUNK, cwb_ref[...], cbb_ref[...]))
        cm = _silu(_conv_rows(lambda s, n: c_ref[0, pl.ds(s, n), :], c, n_chunks, CHUNK, cwc_ref[...], cbc_ref[...]))
        bt = bm.T
        xs_ref[rows, :] = x
        bt_ref[rows, :] = bt
        cs_ref[rows, :] = cm
        pos, dt, tot = _ssd_dt_cols(dt_ref, bias_ref, alog_ref, t0)
        post, dtt = _ssd_dt_rows(dtt_ref, biast_ref, alogt_ref, t0)
        cb = _dot(cm, bt)
        ydiag = []
        for p in range(4):
            wms = []
            for h in (2 * p, 2 * p + 1):
                arg = jnp.where(si <= li,
                                pos[:, h:h + 1] - post[h:h + 1, :],
                                pos[:, 8 + h:9 + h] - post[8 + h:9 + h, :])
                coef = jnp.where(si < li, dtt[h:h + 1, :],
                                 jnp.where(si > li, dtt[8 + h:9 + h, :], dtt[h:h + 1, :] + dtt[8 + h:9 + h, :]))
                wms.append(cb * jnp.exp(arg) * coef)
            xp = x[:, LANES * p:LANES * (p + 1)]
            rhs = jnp.concatenate([jnp.where(lane < 64, xp, 0.0), jnp.where(lane >= 64, xp, 0.0)], axis=0)
            ydiag.append(_dot(jnp.concatenate(wms, axis=1), rhs))
        ht = ht_ref[...]
        yoff = _dot(cm, ht) * _expand_heads(jnp.exp(pos), 0)
        ya_ref[rows, :] = jnp.concatenate(ydiag, axis=1) + yoff
        xw = x * _expand_heads(dt * jnp.exp(tot - pos), 0)
        ht_ref[...] = ht * _expand_heads(jnp.exp(tot), 0) + _dot(bt, xw)
        return carry

    lax.fori_loop(0, n_chunks, fwd, 0)
    store_state(0)

    load_state(1)

    def bwd(i, carry):
        c = n_chunks - 1 - i
        t0 = pl.multiple_of(c * CHUNK, CHUNK)
        rows = pl.ds(t0, CHUNK)
        x = xs_ref[rows, :]
        pos, dt, tot = _ssd_dt_cols(dt_ref, bias_ref, alog_ref, t0)
        ht = ht_ref[...]
        y = ya_ref[rows, :] + _dot(cs_ref[rows, :], ht) * _expand_heads(jnp.exp(pos), 8)
        xw = x * _expand_heads(dt * jnp.exp(tot - pos), 8)
        ht_ref[...] = ht * _expand_heads(jnp.exp(tot), 8) + _dot(bt_ref[rows, :], xw)
        y = (y + dskip_ref[...] * x) * _silu(z_ref[0, rows, :])
        y_ref[0, rows, :] = _rms(y, nw_ref[...]).astype(BF16)
        return carry

    lax.fori_loop(0, n_chunks, bwd, 0)
    store_state(1)


def _ssd_core(main, small, conv_w, conv_b, dt_bias, a_log, d_skip, norm_w, h0, *, batch, seq, emit_state):
    heads, groups, hd = 64, 8, 64
    nstate = 128
    inner = heads * hd
    hp = inner // groups
    hg = heads // groups
    main = main.reshape(batch, seq, -1)
    dt = small.reshape(batch, seq, 2, groups, hg)
    dt_c = dt.transpose(0, 3, 1, 2, 4).reshape(batch, groups, seq, 2 * hg)
    dt_r = dt.transpose(0, 3, 2, 4, 1).reshape(batch, groups, 2 * hg, seq)

    def per_group(p):
        return p.astype(F32).reshape(2, groups, hg).transpose(1, 0, 2).reshape(groups, 2 * hg)

    bias_g, alog_g = per_group(dt_bias), per_group(a_log)
    cw = conv_w.astype(F32)
    cb = conv_b.astype(F32).reshape(1, -1)
    dsk = jnp.repeat(d_skip.astype(F32), hd).reshape(1, inner)
    nw = norm_w.astype(F32).reshape(1, inner)
    xo, bo, co = inner // hp, 2 * inner // nstate, (2 * inner + groups * nstate) // nstate
    in_specs = [
        pl.BlockSpec((1, seq, hp), lambda b, g: (b, 0, g)),
        pl.BlockSpec((1, seq, hp), lambda b, g: (b, 0, xo + g)),
        pl.BlockSpec((1, seq, nstate), lambda b, g: (b, 0, bo + g)),
        pl.BlockSpec((1, seq, nstate), lambda b, g: (b, 0, co + g)),
        pl.BlockSpec((4, hp), lambda b, g: (0, g)),
        pl.BlockSpec((4, nstate), lambda b, g: (0, inner // nstate + g)),
        pl.BlockSpec((4, nstate), lambda b, g: (0, inner // nstate + groups + g)),
        pl.BlockSpec((1, hp), lambda b, g: (0, g)),
        pl.BlockSpec((1, nstate), lambda b, g: (0, inner // nstate + g)),
        pl.BlockSpec((1, nstate), lambda b, g: (0, inner // nstate + groups + g)),
        pl.BlockSpec((1, 1, seq, 2 * hg), lambda b, g: (b, g, 0, 0)),
        pl.BlockSpec((1, 1, 2 * hg, seq), lambda b, g: (b, g, 0, 0)),
        pl.BlockSpec((1, 1, 2 * hg), lambda b, g: (g, 0, 0)),
        pl.BlockSpec((1, 2 * hg, 1), lambda b, g: (g, 0, 0)),
        pl.BlockSpec((1, 1, 2 * hg), lambda b, g: (g, 0, 0)),
        pl.BlockSpec((1, 2 * hg, 1), lambda b, g: (g, 0, 0)),
        pl.BlockSpec((1, hp), lambda b, g: (0, g)),
        pl.BlockSpec((1, hp), lambda b, g: (0, g)),
    ]
    args = [main, main, main, main, cw, cw, cw, cb, cb, cb, dt_c, dt_r,
            bias_g.reshape(groups, 1, 2 * hg), bias_g.reshape(groups, 2 * hg, 1),
            alog_g.reshape(groups, 1, 2 * hg), alog_g.reshape(groups, 2 * hg, 1), dsk, nw]
    state_spec = pl.BlockSpec((1, 2, hg, hd, nstate), lambda b, g: (b, 0, g, 0, 0))
    if h0 is not None:
        in_specs.append(state_spec)
        args.append(h0.astype(F32))
    out_specs = [pl.BlockSpec((1, seq, hp), lambda b, g: (b, 0, g))]
    out_shape = [jax.ShapeDtypeStruct((batch, seq, inner), BF16)]
    if emit_state:
        out_specs.append(state_spec)
        out_shape.append(jax.ShapeDtypeStruct((batch, 2, heads, hd, nstate), F32))
    outs = pl.pallas_call(
        functools.partial(_ssd_kernel, seq=seq, has_h0=h0 is not None, emit_state=emit_state),
        grid=(batch, groups),
        in_specs=in_specs,
        out_specs=out_specs,
        out_shape=out_shape,
        scratch_shapes=[pltpu.VMEM((seq, hp), F32), pltpu.VMEM((seq, nstate), F32), pltpu.VMEM((seq, nstate), F32),
                        pltpu.VMEM((seq, hp), F32), pltpu.VMEM((nstate, hp), F32)],
        compiler_params=_cparams("parallel", "parallel"),
        name="ssd_core",
    )(*args)
    return (outs[0], outs[1]) if emit_state else (outs[0], None)


def _mlstm_kernel(*refs, seq, has_state, emit_state):
    q_ref, k_ref, v_ref, o_ref, gc_ref, gr_ref, bc_ref, br_ref, nw_ref = refs[:9]
    rest = list(refs[9:])
    if has_state:
        c0_ref, n0_ref, m0_ref = rest.pop(0), rest.pop(0), rest.pop(0)
    y_ref = rest.pop(0)
    if emit_state:
        cf_ref, nf_ref, mf_ref = rest.pop(0), rest.pop(0), rest.pop(0)
    hf_ref, ca_ref = rest
    n_chunks = seq // CHUNK
    dqk = q_ref.shape[2]
    dv = v_ref.shape[2]
    li = lax.broadcasted_iota(jnp.int32, (CHUNK, CHUNK), 0)
    si = lax.broadcasted_iota(jnp.int32, (CHUNK, CHUNK), 1)
    lane = lax.broadcasted_iota(jnp.int32, (CHUNK, LANES), 1)
    ones_col = (lane == 0).astype(F32)
    k_scale = dqk ** -0.5

    def load_state(d):
        if has_state:
            ncol = jnp.where(lane[:dqk] == 0, n0_ref[0, d, 0], 0.0)
            ca_ref[...] = jnp.concatenate([c0_ref[0, d, 0], ncol], axis=1)
            return m0_ref[0, d, 0]
        ca_ref[...] = jnp.zeros_like(ca_ref)
        return jnp.zeros((1, 1), F32)

    def store_state(d, m):
        if emit_state:
            cf_ref[0, d, 0] = ca_ref[:, :dv]
            nf_ref[0, d, 0] = ca_ref[:, dv:dv + 1]
            mf_ref[0, d, 0] = m

    def gates(t0):
        gc = gc_ref[0, 0, pl.ds(t0, CHUNK), :] + bc_ref[0]
        gr = gr_ref[0, 0, :, pl.ds(t0, CHUNK)] + br_ref[0]
        ci = lax.broadcasted_iota(jnp.int32, gc.shape, 1)
        ri = lax.broadcasted_iota(jnp.int32, gr.shape, 0)
        gc = jnp.where((ci == 1) | (ci == 3), -_softplus(-gc), gc)
        gr = jnp.where((ri == 1) | (ri == 3), -_softplus(-gr), gr)
        return gc, gr, _dot_exact(_tri(CHUNK, True), gc), _dot_exact(gr, _tri(CHUNK, False))

    def direction(q, kt, v, vaug, qk, cum_col, neg_row, tot, mask, m):
        dmat = jnp.where(mask, cum_col + neg_row, -jnp.inf)
        inter = cum_col + m
        mt = jnp.maximum(inter, jnp.max(dmat, axis=1, keepdims=True))
        s = qk * jnp.exp(dmat - mt)
        w_in = jnp.exp(inter - mt)
        ca = ca_ref[...]
        qc = _dot(q, ca)
        num = _dot(s, v) + qc[:, :dv] * w_in
        den = jnp.sum(s, axis=1, keepdims=True) + qc[:, dv:dv + 1] * w_in
        hout = num / jnp.maximum(jnp.abs(den), jnp.exp(-mt))
        gl = tot + neg_row
        m_new = jnp.maximum(tot + m, jnp.max(gl, axis=1, keepdims=True))
        ws = jnp.exp(gl - m_new)
        dec = jnp.exp(tot + m - m_new)
        ca_ref[...] = dec * ca + _dot(kt * ws, vaug)
        return hout, m_new

    def chunk_operands(rows):
        q = q_ref[0, rows, :]
        kt = (k_ref[0, rows, :] * k_scale).T
        v = v_ref[0, rows, :]
        return q, kt, v, jnp.concatenate([v, ones_col], axis=1), _dot(q, kt)

    m = load_state(0)

    def fwd(c, m):
        t0 = pl.multiple_of(c * CHUNK, CHUNK)
        rows = pl.ds(t0, CHUNK)
        q, kt, v, vaug, qk = chunk_operands(rows)
        gc, gr, cum_c, cum_r = gates(t0)
        tot = cum_r[1:2, CHUNK - 1:CHUNK]
        hout, m = direction(q, kt, v, vaug, qk, cum_c[:, 1:2], gr[0:1] - cum_r[1:2], tot, si <= li, m)
        hf_ref[rows, :] = hout
        return m

    m = lax.fori_loop(0, n_chunks, fwd, m)
    store_state(0, m)
    m = load_state(1)

    def bwd(i, m):
        c = n_chunks - 1 - i
        t0 = pl.multiple_of(c * CHUNK, CHUNK)
        rows = pl.ds(t0, CHUNK)
        q, kt, v, vaug, qk = chunk_operands(rows)
        gc, gr, cum_c, cum_r = gates(t0)
        tot = cum_r[3:4, CHUNK - 1:CHUNK]
        suf_col = tot - cum_c[:, 3:4] + gc[:, 3:4]
        suf_row = tot - cum_r[3:4] + gr[3:4]
        hout, m = direction(q, kt, v, vaug, qk, suf_col, gr[2:3] - suf_row, tot, si >= li, m)
        h = _rms(hf_ref[rows, :] + hout, nw_ref[...])
        y_ref[0, rows, :] = (h * _sigmoid(o_ref[0, rows, :])).astype(BF16)
        return m

    m = lax.fori_loop(0, n_chunks, bwd, m)
    store_state(1, m)


def _mlstm_core(main, small, igate_b, fgate_b, norm_w, state, *, batch, seq, emit_state):
    heads, dqk, dv = 8, 128, 256
    main = main.reshape(batch, seq, -1)
    g = small[:, :4 * heads].reshape(batch, seq, 4, heads)
    gc = jnp.pad(g.transpose(0, 3, 1, 2), ((0, 0), (0, 0), (0, 0), (0, 4)))
    gr = jnp.pad(g.transpose(0, 3, 2, 1), ((0, 0), (0, 0), (0, 4), (0, 0)))
    bias = jnp.stack([igate_b[0], fgate_b[0], igate_b[1], fgate_b[1]], axis=1).astype(F32)
    bias = jnp.pad(bias, ((0, 0), (0, 4)))
    in_specs = [
        pl.BlockSpec((1, seq, dqk), lambda b, h: (b, 0, h)),
        pl.BlockSpec((1, seq, dqk), lambda b, h: (b, 0, heads + h)),
        pl.BlockSpec((1, seq, dv), lambda b, h: (b, 0, heads + h)),
        pl.BlockSpec((1, seq, dv), lambda b, h: (b, 0, 2 * heads + h)),
        pl.BlockSpec((1, 1, seq, 8), lambda b, h: (b, h, 0, 0)),
        pl.BlockSpec((1, 1, 8, seq), lambda b, h: (b, h, 0, 0)),
        pl.BlockSpec((1, 1, 8), lambda b, h: (h, 0, 0)),
        pl.BlockSpec((1, 8, 1), lambda b, h: (h, 0, 0)),
        pl.BlockSpec((1, dv), lambda b, h: (0, h)),
    ]
    args = [main, main, main, main, gc, gr, bias.reshape(heads, 1, 8), bias.reshape(heads, 8, 1),
            norm_w.astype(F32).reshape(1, heads * dv)]
    c_spec = pl.BlockSpec((1, 2, 1, dqk, dv), lambda b, h: (b, 0, h, 0, 0))
    n_spec = pl.BlockSpec((1, 2, 1, dqk, 1), lambda b, h: (b, 0, h, 0, 0))
    m_spec = pl.BlockSpec((1, 2, 1, 1, 1), lambda b, h: (b, 0, h, 0, 0))
    if state is not None:
        c0, n0, m0 = state
        in_specs += [c_spec, n_spec, m_spec]
        args += [c0.astype(F32), n0.astype(F32).reshape(batch, 2, heads, dqk, 1),
                 m0.astype(F32).reshape(batch, 2, heads, 1, 1)]
    out_specs = [pl.BlockSpec((1, seq, dv), lambda b, h: (b, 0, h))]
    out_shape = [jax.ShapeDtypeStruct((batch, seq, heads * dv), BF16)]
    if emit_state:
        out_specs += [c_spec, n_spec, m_spec]
        out_shape += [jax.ShapeDtypeStruct((batch, 2, heads, dqk, dv), F32),
                      jax.ShapeDtypeStruct((batch, 2, heads, dqk, 1), F32),
                      jax.ShapeDtypeStruct((batch, 2, heads, 1, 1), F32)]
    outs = pl.pallas_call(
        functools.partial(_mlstm_kernel, seq=seq, has_state=state is not None, emit_state=emit_state),
        grid=(batch, heads),
        in_specs=in_specs,
        out_specs=out_specs,
        out_shape=out_shape,
        scratch_shapes=[pltpu.VMEM((seq, dv), F32), pltpu.VMEM((dqk, dv + LANES), F32)],
        compiler_params=_cparams("parallel", "parallel"),
        name="mlstm_core",
    )(*args)
    if emit_state:
        return outs[0], (outs[1], outs[2].reshape(batch, 2, heads, dqk), outs[3].reshape(batch, 2, heads))
    return outs[0], None


def _scan8(a, b, reverse):
    row = lax.broadcasted_iota(jnp.int32, a.shape, 0)
    for k in (1, 2, 4):
        shift = SUBLANES - k if reverse else k
        valid = row < SUBLANES - k if reverse else row >= k
        a_prev = pltpu.roll(a, shift, 0)
        b_prev = pltpu.roll(b, shift, 0)
        b = jnp.where(valid, a * b_prev, 0.0) + b
        a = jnp.where(valid, a * a_prev, a)
    return a, b


def _lru_kernel(*refs, seq, rows, has_h0, emit_state):
    (gate_ref, x_ref, cw_ref, cb_ref, wa_ref, wi_ref, ba_ref, bi_ref, lam_ref) = refs[:9]
    rest = list(refs[9:])
    h0_ref = rest.pop(0) if has_h0 else None
    y_ref = rest.pop(0)
    st_ref = rest.pop(0) if emit_state else None
    hf_ref, a_ref, b_ref, hb_ref = rest
    n_chunks = seq // rows
    n_groups = rows // SUBLANES
    width = x_ref.shape[2]

    def coefficients(c, d):
        xc = _conv_rows(lambda s, n: x_ref[0, pl.ds(s, n), :], c, n_chunks, rows, cw_ref[...], cb_ref[...])
        r = _sigmoid(_dot(xc, wa_ref[d, 0]) + ba_ref[d])
        i = _sigmoid(_dot(xc, wi_ref[d, 0]) + bi_ref[d])
        log_a = -LRU_C * r * _softplus(-lam_ref[d])
        a = jnp.exp(log_a)
        a_ref[...] = a
        b_ref[...] = jnp.sqrt(-jnp.tanh(log_a) * (1.0 + a * a)) * (i * xc)

    def initial(d):
        return h0_ref[0, d] if has_h0 else jnp.zeros((1, width), F32)

    def scan_chunk(out_ref, base, carry, reverse):
        def group(i, carry):
            g = n_groups - 1 - i if reverse else i
            r0 = pl.multiple_of(g * SUBLANES, SUBLANES)
            a, b = _scan8(a_ref[pl.ds(r0, SUBLANES), :], b_ref[pl.ds(r0, SUBLANES), :], reverse)
            h = b + a * carry
            out_ref[pl.ds(pl.multiple_of(base + r0, SUBLANES), SUBLANES), :] = h
            return h[0:1] if reverse else h[SUBLANES - 1:SUBLANES]
        return lax.fori_loop(0, n_groups, group, carry)

    def fwd(c, carry):
        coefficients(c, 0)
        return scan_chunk(hf_ref, c * rows, carry, False)

    carry = lax.fori_loop(0, n_chunks, fwd, initial(0))
    if emit_state:
        st_ref[0, 0] = carry

    def bwd(i, carry):
        c = n_chunks - 1 - i
        coefficients(c, 1)
        carry = scan_chunk(hb_ref, 0, carry, True)
        sl = pl.ds(pl.multiple_of(c * rows, rows), rows)
        y = (hf_ref[sl, :] + hb_ref[...]) * jax.nn.gelu(gate_ref[0, sl, :], approximate=True)
        y_ref[0, sl, :] = y.astype(BF16)
        return carry

    carry = lax.fori_loop(0, n_chunks, bwd, initial(1))
    if emit_state:
        st_ref[0, 1] = carry


def _lru_core(main, conv_w, conv_b, wa, ba, wi, bi, lam, h0, *, batch, seq, emit_state):
    nb, bs = wa.shape[1], wa.shape[2]
    width = nb * bs
    rows = min(seq, 256)
    main = main.reshape(batch, seq, -1)
    vec = lambda p: p.astype(F32).reshape(2, 1, width)
    in_specs = [
        pl.BlockSpec((1, seq, bs), lambda b, n: (b, 0, n)),
        pl.BlockSpec((1, seq, bs), lambda b, n: (b, 0, nb + n)),
        pl.BlockSpec((4, bs), lambda b, n: (0, n)),
        pl.BlockSpec((1, bs), lambda b, n: (0, n)),
        pl.BlockSpec((2, 1, bs, bs), lambda b, n: (0, n, 0, 0)),
        pl.BlockSpec((2, 1, bs, bs), lambda b, n: (0, n, 0, 0)),
        pl.BlockSpec((2, 1, bs), lambda b, n: (0, 0, n)),
        pl.BlockSpec((2, 1, bs), lambda b, n: (0, 0, n)),
        pl.BlockSpec((2, 1, bs), lambda b, n: (0, 0, n)),
    ]
    args = [main, main, conv_w.astype(F32), conv_b.astype(F32).reshape(1, width), wa, wi, vec(ba), vec(bi), vec(lam)]
    state_spec = pl.BlockSpec((1, 2, 1, bs), lambda b, n: (b, 0, 0, n))
    if h0 is not None:
        in_specs.append(state_spec)
        args.append(h0.astype(F32).reshape(batch, 2, 1, width))
    out_specs = [pl.BlockSpec((1, seq, bs), lambda b, n: (b, 0, n))]
    out_shape = [jax.ShapeDtypeStruct((batch, seq, width), BF16)]
    if emit_state:
        out_specs.append(state_spec)
        out_shape.append(jax.ShapeDtypeStruct((batch, 2, 1, width), F32))
    outs = pl.pallas_call(
        functools.partial(_lru_kernel, seq=seq, rows=rows, has_h0=h0 is not None, emit_state=emit_state),
        grid=(batch, nb),
        in_specs=in_specs,
        out_specs=out_specs,
        out_shape=out_shape,
        scratch_shapes=[pltpu.VMEM((seq, bs), F32), pltpu.VMEM((rows, bs), F32), pltpu.VMEM((rows, bs), F32),
                        pltpu.VMEM((rows, bs), F32)],
        compiler_params=_cparams("parallel", "parallel"),
        name="lru_core",
    )(*args)
    return (outs[0], outs[1].reshape(batch, 2, width)) if emit_state else (outs[0], None)


def _to_col_major(h):
    b, l, d = h.shape
    return h.reshape(b, l // GRID_W, GRID_W, d).transpose(0, 2, 1, 3).reshape(b, l, d)


def _to_row_major(h):
    b, l, d = h.shape
    return h.reshape(b, GRID_W, l // GRID_W, d).transpose(0, 2, 1, 3).reshape(b, l, d)


def _trunk(x, mod, grid, states, p, w):
    batch, seq, _ = x.shape
    depth = mod.shape[0]
    emit = states is None
    finals = dict(ssd=[], ml_c=[], ml_n=[], ml_m=[], lru=[])
    for l in range(depth):
        sh1, sc1, g1, sh2, sc2, g2 = (mod[l, :, k][:, None, :] for k in range(6))
        kind, j = l % N_MIXERS, l // N_MIXERS
        col = grid and j % 2 == 1
        xin = _to_col_major(x) if col else x
        if kind == 0:
            main, small = _in_proj(xin, p["norm_mix_pre"][l], sc1, sh1, w["ssd_in_main"][j], w["ssd_in_dt"][j])
            y, s = _ssd_core(main, small, p["ssd_conv_w"][j], p["ssd_conv_b"][j], p["ssd_dt_bias"][j],
                             p["ssd_a_log"][j], p["ssd_d"][j], p["ssd_norm_w"][j],
                             None if emit else states["ssd"][:, j], batch=batch, seq=seq, emit_state=emit)
            finals["ssd"].append(s)
            w_out = w["ssd_out"][j]
        elif kind == 1:
            main, small = _in_proj(xin, p["norm_mix_pre"][l], sc1, sh1, w["ml_in_main"][j], w["ml_in_gates"][j])
            st = None if emit else (states["ml_c"][:, j], states["ml_n"][:, j], states["ml_m"][:, j])
            y, s = _mlstm_core(main, small, p["ml_igate_b"][j], p["ml_fgate_b"][j], p["ml_norm_w"][j], st,
                               batch=batch, seq=seq, emit_state=emit)
            if emit:
                finals["ml_c"].append(s[0])
                finals["ml_n"].append(s[1])
                finals["ml_m"].append(s[2])
            w_out = w["ml_out"][j]
        else:
            main, _ = _in_proj(xin, p["norm_mix_pre"][l], sc1, sh1, w["lru_in"][j])
            y, s = _lru_core(main, p["lru_conv_w"][j], p["lru_conv_b"][j], w["lru_wa"][j], p["lru_ba"][j],
                             w["lru_wi"][j], p["lru_bi"][j], p["lru_lambda"][j],
                             None if emit else states["lru"][:, j], batch=batch, seq=seq, emit_state=emit)
            finals["lru"].append(s)
            w_out = w["lru_out"][j]
        xo = _out_proj(y, w_out, xin, g1, p["norm_mix_post"][l])
        x = _to_row_major(xo) if col else xo
        x = _mlp(x, p["norm_mlp_pre"][l], sc2, sh2, w["mlp_w1"][l], w["mlp_w2"][l], g2, p["norm_mlp_post"][l])
    if emit:
        return x, {k: jnp.stack(v, axis=1) for k, v in finals.items()}
    return x, None


def kernel(x_prompt, x_sample, state_ssd, state_mlstm_C, state_mlstm_n, state_mlstm_m, state_rglru, c, c_ctx, mod_w, mod_b, norm_mix_pre, norm_mix_post, norm_mlp_pre, norm_mlp_post, mlp_w1, mlp_w2, ssd_in_w, ssd_conv_w, ssd_conv_b, ssd_dt_bias, ssd_a_log, ssd_d, ssd_norm_w, ssd_out_w, ml_in_w, ml_igate_b, ml_fgate_b, ml_norm_w, ml_out_w, lru_in_w, lru_conv_w, lru_conv_b, lru_wa, lru_ba, lru_wi, lru_bi, lru_lambda, lru_out_w):
    p = dict(norm_mix_pre=norm_mix_pre, norm_mix_post=norm_mix_post, norm_mlp_pre=norm_mlp_pre,
             norm_mlp_post=norm_mlp_post, ssd_conv_w=ssd_conv_w, ssd_conv_b=ssd_conv_b, ssd_dt_bias=ssd_dt_bias,
             ssd_a_log=ssd_a_log, ssd_d=ssd_d, ssd_norm_w=ssd_norm_w, ml_igate_b=ml_igate_b, ml_fgate_b=ml_fgate_b,
             ml_norm_w=ml_norm_w, lru_conv_w=lru_conv_w, lru_conv_b=lru_conv_b, lru_ba=lru_ba, lru_bi=lru_bi,
             lru_lambda=lru_lambda)
    d_model = x_prompt.shape[-1]
    depth = mod_w.shape[0]
    ssd_inner = ssd_out_w.shape[1]
    ssd_main = ssd_in_w.shape[2] - LANES
    ml_main = ml_in_w.shape[2] - 4 * state_mlstm_m.shape[-1]
    bf = lambda a: a.astype(BF16)
    w = dict(
        mlp_w1=bf(mlp_w1), mlp_w2=bf(mlp_w2),
        ssd_in_main=bf(ssd_in_w[:, :, :ssd_main]), ssd_in_dt=bf(ssd_in_w[:, :, ssd_main:]), ssd_out=bf(ssd_out_w),
        ml_in_main=bf(ml_in_w[:, :, :ml_main]),
        ml_in_gates=bf(jnp.pad(ml_in_w[:, :, ml_main:], ((0, 0), (0, 0), (0, LANES - (ml_in_w.shape[2] - ml_main))))),
        ml_out=bf(ml_out_w), lru_in=bf(lru_in_w), lru_out=bf(lru_out_w), lru_wa=bf(lru_wa), lru_wi=bf(lru_wi),
    )
    assert ssd_inner == 4096
    n_dec = c.shape[0]
    n_rows = -(-(1 + n_dec) // SUBLANES) * SUBLANES
    cond = jnp.concatenate([c_ctx[None].astype(F32), c.astype(F32),
                            jnp.zeros((n_rows - 1 - n_dec, d_model), F32)], axis=0)
    mod = _mod_all(cond, mod_w, mod_b).reshape(depth, n_rows, 6, d_model)
    y_prompt, fin = _trunk(x_prompt, mod[:, 0:1], False, None, p, w)
    states = dict(ssd=state_ssd, ml_c=state_mlstm_C, ml_n=state_mlstm_n, ml_m=state_mlstm_m, lru=state_rglru)
    y_sample, _ = _trunk(x_sample, mod[:, 1:1 + n_dec], True, states, p, w)
    dt = x_prompt.dtype
    return (y_prompt, y_sample, fin["ssd"].astype(dt), fin["ml_c"].astype(dt), fin["ml_n"].astype(dt),
            fin["ml_m"].astype(dt), fin["lru"].astype(dt))
```

```python
import functools

import jax
import jax.numpy as jnp
from jax import lax
from jax.experimental import pallas as pl
from jax.experimental.pallas import tpu as pltpu

F32 = jnp.float32
BF16 = jnp.bfloat16
HIGHEST = lax.Precision.HIGHEST

EPS = 1e-6
CHUNK = 128
GRID_W = 64
N_MIXERS = 3
LRU_C = 8.0
SUBLANES = 8
LANES = 128
HALO = 16
VMEM_LIMIT_BYTES = 56 * 1024 * 1024


def _cparams(*semantics):
    return pltpu.CompilerParams(dimension_semantics=semantics, vmem_limit_bytes=VMEM_LIMIT_BYTES)


def _dot(a, b):
    return jnp.dot(a.astype(BF16), b.astype(BF16), preferred_element_type=F32)


def _dot_exact(a, b):
    return jnp.dot(a, b, precision=HIGHEST, preferred_element_type=F32)


def _softplus(x):
    return jnp.maximum(x, 0.0) + jnp.log1p(jnp.exp(-jnp.abs(x)))


def _sigmoid(x):
    return jax.nn.sigmoid(x)


def _silu(x):
    return x * jax.nn.sigmoid(x)


def _rms(x, w):
    return x * lax.rsqrt(jnp.mean(x * x, axis=-1, keepdims=True) + EPS) * w


def _tri(n, lower):
    r = lax.broadcasted_iota(jnp.int32, (n, n), 0)
    c = lax.broadcasted_iota(jnp.int32, (n, n), 1)
    return (r >= c if lower else r <= c).astype(F32)


def _split3(v):
    hi = v.astype(BF16).astype(F32)
    r = v - hi
    mid = r.astype(BF16).astype(F32)
    return hi, mid, r - mid


_ACTS = {
    "none": lambda v: v,
    "silu": _silu,
    "sigmoid": _sigmoid,
    "gelu": lambda v: jax.nn.gelu(v, approximate=True),
}


def _mod_kernel(c_ref, w_ref, b_ref, o_ref):
    a = _silu(c_ref[...])
    o_ref[0] = _dot(a, w_ref[0]) + b_ref[0]


def _mod_all(cond, mod_w, mod_b):
    depth, d, n = mod_w.shape
    r = cond.shape[0]
    tn = 1024
    return pl.pallas_call(
        _mod_kernel,
        grid=(depth, n // tn),
        in_specs=[
            pl.BlockSpec((r, d), lambda l, j: (0, 0)),
            pl.BlockSpec((1, d, tn), lambda l, j: (l, 0, j)),
            pl.BlockSpec((1, 1, tn), lambda l, j: (l, 0, j)),
        ],
        out_specs=pl.BlockSpec((1, r, tn), lambda l, j: (l, 0, j)),
        out_shape=jax.ShapeDtypeStruct((depth, r, n), F32),
        compiler_params=_cparams("parallel", "parallel"),
        name="mod_map",
    )(cond, mod_w, mod_b.reshape(depth, 1, n))


def _in_proj_kernel(*refs, kinds, seq, tm, halo, has_conv, has_small):
    refs = list(refs)
    x_ref = refs.pop(0)
    xp_ref, xn_ref = (refs.pop(0), refs.pop(0)) if halo else (None, None)
    nw_ref, sc_ref, sh_ref, w_ref = refs.pop(0), refs.pop(0), refs.pop(0), refs.pop(0)
    cw_ref, cb_ref = (refs.pop(0), refs.pop(0)) if has_conv else (None, None)
    ws_ref = refs.pop(0) if has_small else None
    o_ref = refs.pop(0)
    os_ref = refs.pop(0) if has_small else None
    (h_ref,) = refs
    i, j = pl.program_id(0), pl.program_id(1)
    pad = HALO if halo else 0
    main = pl.ds(pad, tm)

    @pl.when(j == 0)
    def _():
        norm = lambda v: (_rms(v, nw_ref[...]) * (1.0 + sc_ref[0]) + sh_ref[0]).astype(BF16)
        h_ref[main, :] = norm(x_ref[...])
        if halo:
            h_ref[pl.ds(0, pad), :] = norm(xp_ref[...])
            h_ref[pl.ds(pad + tm, pad), :] = norm(xn_ref[...])
        if has_small:
            os_ref[...] = jnp.dot(h_ref[main, :], ws_ref[...], preferred_element_type=F32)

    def plain(act):
        o_ref[...] = act(jnp.dot(h_ref[main, :], w_ref[...], preferred_element_type=F32))

    def conv(act):
        r = jnp.dot(h_ref[...], w_ref[...], preferred_element_type=F32)
        n = r.shape[0]
        if halo:
            lead = jnp.where((i * tm) % seq == 0, 0.0, r[:pad])
            tail = jnp.where(((i + 1) * tm) % seq == 0, 0.0, r[pad + tm:])
            r = jnp.concatenate([lead, r[pad:pad + tm], tail], axis=0)
        tap = lambda shift: pltpu.roll(r, shift % n, 0)[pad:pad + tm]
        t2, t1, tn1 = tap(2), tap(1), tap(-1)
        if not halo:
            pos = lax.broadcasted_iota(jnp.int32, r.shape, 0) & (seq - 1)
            t2 = jnp.where(pos >= 2, t2, 0.0)
            t1 = jnp.where(pos >= 1, t1, 0.0)
            tn1 = jnp.where(pos < seq - 1, tn1, 0.0)
        w = cw_ref[...]
        o_ref[...] = act(w[0:1] * t2 + w[1:2] * t1 + w[2:3] * r[pad:pad + tm] + w[3:4] * tn1 + cb_ref[...])

    start = 0
    for count, kind in kinds:
        body = (functools.partial(conv, _ACTS[kind[5:]]) if kind.startswith("conv_")
                else functools.partial(plain, _ACTS[kind]))
        pl.when((j >= start) & (j < start + count))(body)
        start += count


def _in_proj(x, norm_w, scale, shift, w_main, w_small=None, *, kinds, conv_w=None, conv_b=None, tm=1024, tn=512):
    b, l, d = x.shape
    m = b * l
    n = w_main.shape[1]
    per_batch = scale.shape[0] > 1
    assert m % tm == 0 and n % tn == 0 and (not per_batch or l % tm == 0)
    assert sum(c for c, _ in kinds) == n // tn
    has_conv = any(k.startswith("conv_") for _, k in kinds)
    halo = has_conv and l > tm
    assert not has_conv or (l % tm == 0 if halo else (tm % l == 0 and l & (l - 1) == 0))
    mod_idx = (lambda i, j: ((i * tm) // l, 0, 0)) if per_batch else (lambda i, j: (0, 0, 0))
    has_small = w_small is not None
    x2 = x.reshape(m, d)
    in_specs = [pl.BlockSpec((tm, d), lambda i, j: (i, 0))]
    args = [x2]
    if halo:
        per_tile, last = tm // HALO, m // HALO - 1
        in_specs += [pl.BlockSpec((HALO, d), lambda i, j: (jnp.maximum(i * per_tile - 1, 0), 0)),
                     pl.BlockSpec((HALO, d), lambda i, j: (jnp.minimum((i + 1) * per_tile, last), 0))]
        args += [x2, x2]
    in_specs += [
        pl.BlockSpec((1, d), lambda i, j: (0, 0)),
        pl.BlockSpec((1, 1, d), mod_idx),
        pl.BlockSpec((1, 1, d), mod_idx),
        pl.BlockSpec((d, tn), lambda i, j: (0, j)),
    ]
    args += [norm_w.reshape(1, d), scale, shift, w_main]
    if has_conv:
        in_specs += [pl.BlockSpec((4, tn), lambda i, j: (0, j)), pl.BlockSpec((1, tn), lambda i, j: (0, j))]
        args += [conv_w.astype(F32), conv_b.astype(F32).reshape(1, n)]
    out_specs = [pl.BlockSpec((tm, tn), lambda i, j: (i, j))]
    out_shape = [jax.ShapeDtypeStruct((m, n), F32)]
    if has_small:
        ns = w_small.shape[1]
        in_specs.append(pl.BlockSpec((d, ns), lambda i, j: (0, 0)))
        out_specs.append(pl.BlockSpec((tm, ns), lambda i, j: (i, 0)))
        out_shape.append(jax.ShapeDtypeStruct((m, ns), F32))
        args.append(w_small)
    outs = pl.pallas_call(
        functools.partial(_in_proj_kernel, kinds=tuple(kinds), seq=l, tm=tm, halo=halo, has_conv=has_conv,
                          has_small=has_small),
        grid=(m // tm, n // tn),
        in_specs=in_specs,
        out_specs=out_specs,
        out_shape=out_shape,
        scratch_shapes=[pltpu.VMEM((tm + (2 * HALO if halo else 0), d), BF16)],
        compiler_params=_cparams("parallel", "arbitrary"),
        name="in_proj",
    )(*args)
    return (outs[0], outs[1]) if has_small else (outs[0], None)


def _out_proj_kernel(y_ref, w_ref, x_ref, g_ref, nw_ref, o_ref, *acc, nk):
    k = pl.program_id(1)
    part = jnp.dot(y_ref[...], w_ref[...], preferred_element_type=F32)
    finish = lambda f: x_ref[...] + g_ref[0] * _rms(f, nw_ref[...])
    if nk == 1:
        o_ref[...] = finish(part)
        return
    (acc_ref,) = acc

    @pl.when(k == 0)
    def _():
        acc_ref[...] = part

    @pl.when((k > 0) & (k < nk - 1))
    def _():
        acc_ref[...] += part

    @pl.when(k == nk - 1)
    def _():
        o_ref[...] = finish(acc_ref[...] + part)


def _out_proj(y, w, x, gate, norm_w, *, tm=512, tk=2048):
    b, l, d = x.shape
    m = b * l
    kin = w.shape[0]
    per_batch = gate.shape[0] > 1
    assert m % tm == 0 and kin % tk == 0 and (not per_batch or l % tm == 0)
    mod_idx = (lambda i, k: ((i * tm) // l, 0, 0)) if per_batch else (lambda i, k: (0, 0, 0))
    nk = kin // tk
    out = pl.pallas_call(
        functools.partial(_out_proj_kernel, nk=nk),
        grid=(m // tm, nk),
        in_specs=[
            pl.BlockSpec((tm, tk), lambda i, k: (i, k)),
            pl.BlockSpec((tk, d), lambda i, k: (k, 0)),
            pl.BlockSpec((tm, d), lambda i, k: (i, 0)),
            pl.BlockSpec((1, 1, d), mod_idx),
            pl.BlockSpec((1, d), lambda i, k: (0, 0)),
        ],
        out_specs=pl.BlockSpec((tm, d), lambda i, k: (i, 0)),
        out_shape=jax.ShapeDtypeStruct((m, d), F32),
        scratch_shapes=[pltpu.VMEM((tm, d), F32)] if nk > 1 else [],
        compiler_params=_cparams("parallel", "arbitrary"),
        name="out_proj",
    )(y.reshape(m, kin), w, x.reshape(m, d), gate, norm_w.reshape(1, d))
    return out.reshape(b, l, d)


def _mlp_kernel(x_ref, nw_ref, sc_ref, sh_ref, w1_ref, w2_ref, g_ref, pw_ref, o_ref, h_ref, acc_ref, *, nf):
    j = pl.program_id(1)

    @pl.when(j == 0)
    def _():
        h = _rms(x_ref[...], nw_ref[...]) * (1.0 + sc_ref[0]) + sh_ref[0]
        h_ref[...] = h.astype(BF16)
        acc_ref[...] = jnp.zeros_like(acc_ref)

    t = jnp.dot(h_ref[...], w1_ref[...], preferred_element_type=F32)
    t = jnp.square(jnp.maximum(t, 0.0))
    acc_ref[...] += jnp.dot(t.astype(BF16), w2_ref[...], preferred_element_type=F32)

    @pl.when(j == nf - 1)
    def _():
        o_ref[...] = x_ref[...] + g_ref[0] * _rms(acc_ref[...], pw_ref[...])


def _mlp(x, pre_w, scale, shift, w1, w2, gate, post_w, *, tm=512, tf=512):
    b, l, d = x.shape
    m = b * l
    dff = w1.shape[1]
    per_batch = scale.shape[0] > 1
    assert m % tm == 0 and dff % tf == 0 and (not per_batch or l % tm == 0)
    mod_idx = (lambda i, j: ((i * tm) // l, 0, 0)) if per_batch else (lambda i, j: (0, 0, 0))
    nf = dff // tf
    out = pl.pallas_call(
        functools.partial(_mlp_kernel, nf=nf),
        grid=(m // tm, nf),
        in_specs=[
            pl.BlockSpec((tm, d), lambda i, j: (i, 0)),
            pl.BlockSpec((1, d), lambda i, j: (0, 0)),
            pl.BlockSpec((1, 1, d), mod_idx),
            pl.BlockSpec((1, 1, d), mod_idx),
            pl.BlockSpec((d, tf), lambda i, j: (0, j)),
            pl.BlockSpec((tf, d), lambda i, j: (j, 0)),
            pl.BlockSpec((1, 1, d), mod_idx),
            pl.BlockSpec((1, d), lambda i, j: (0, 0)),
        ],
        out_specs=pl.BlockSpec((tm, d), lambda i, j: (i, 0)),
        out_shape=jax.ShapeDtypeStruct((m, d), F32),
        scratch_shapes=[pltpu.VMEM((tm, d), BF16), pltpu.VMEM((tm, d), F32)],
        compiler_params=_cparams("parallel", "arbitrary"),
        name="mlp",
    )(x.reshape(m, d), pre_w.reshape(1, d), scale, shift, w1, w2, gate, post_w.reshape(1, d))
    return out.reshape(b, l, d)


SSD_GROUP_HEADS = 8
SSD_HEAD_DIM = 64


def _ssd_kernel(*refs, seq, has_h0, emit_state):
    z_ref, x_ref, b_ref, c_ref, dtt_ref, biast_ref, alogt_ref, dskip_ref, nw_ref = refs[:9]
    rest = list(refs[9:])
    h0_ref = rest.pop(0) if has_h0 else None
    y_ref = rest.pop(0)
    st_ref = rest.pop(0) if emit_state else None
    bt_ref, ya_ref, htf_ref, htb_ref, col_ref, qry_ref, keyf_ref, keyb_ref, dts_ref = rest
    n_chunks = seq // CHUNK
    gh, hd = SSD_GROUP_HEADS, SSD_HEAD_DIM
    hp = gh * hd
    wide = gh * CHUNK
    li = lax.broadcasted_iota(jnp.int32, (CHUNK, CHUNK), 0)
    si = lax.broadcasted_iota(jnp.int32, (CHUNK, CHUNK), 1)
    lane = lax.broadcasted_iota(jnp.int32, (CHUNK, LANES), 1)

    onehot = lambda cond: jnp.where(cond, 1.0, 0.0).astype(BF16)
    k_i = lax.broadcasted_iota(jnp.int32, (48, wide), 0)
    head_of_lane = lax.broadcasted_iota(jnp.int32, (48, wide), 1) >> 7
    pick_f = onehot((k_i & 15) == head_of_lane)
    pick_b = onehot((k_i & 15) == gh + head_of_lane)
    zero_tail = jnp.zeros((48, wide), BF16)
    diag_blocks = (lax.broadcasted_iota(jnp.int32, (gh, wide), 0)
                   == lax.broadcasted_iota(jnp.int32, (gh, wide), 1) >> 7)
    e_k = lax.broadcasted_iota(jnp.int32, (CHUNK, 2 * hp), 0)
    e_j = lax.broadcasted_iota(jnp.int32, (CHUNK, 2 * hp), 1)
    e_head = (e_j & (hp - 1)) >> 6
    in_rows = ((e_j < hp) & (e_k < 48)) | ((e_j >= hp) & (e_k >= 48) & (e_k < 96))
    spread = lambda d: onehot(in_rows & ((e_k & 15) == d + e_head))
    spread_f, spread_b = spread(0), spread(gh)

    def key_rows(parts, d):
        blocks = [jnp.where(diag_blocks, -jnp.tile(p[d:d + gh], (1, gh)), 0.0) for p in parts]
        blocks.append(jnp.zeros((gh, wide), F32))
        return jnp.concatenate(blocks, axis=0).astype(BF16)

    dt_all = _softplus(dtt_ref[0, 0] + biast_ref[0])
    dts_ref[...] = dt_all
    stack = lambda v: jnp.concatenate([v[:, CHUNK * c:CHUNK * (c + 1)] for c in range(n_chunks)], axis=0)
    dt_st = stack(dt_all)
    a_st = stack(dt_all * (-jnp.exp(alogt_ref[0])))
    cs = _dot_exact(a_st, _tri(CHUNK, False))
    tot = cs[:, CHUNK - 1:CHUNK]
    backward = (lax.broadcasted_iota(jnp.int32, cs.shape, 0) & (2 * gh - 1)) >= gh
    pos = jnp.where(backward, tot - cs + a_st, cs)
    pos3, exp3, wgt3 = _split3(pos), _split3(jnp.exp(pos)), _split3(dt_st * jnp.exp(tot - pos))
    for c in range(n_chunks):
        blk, tok = slice(2 * gh * c, 2 * gh * (c + 1)), slice(CHUNK * c, CHUNK * (c + 1))
        parts = [p[blk] for p in pos3]
        qry_ref[tok, :] = jnp.concatenate([*parts, jnp.ones((32, CHUNK), F32), jnp.zeros((48, CHUNK), F32)],
                                          axis=0).T.astype(BF16)
        col_ref[tok, :] = jnp.concatenate([*(e[blk] for e in exp3), *(w[blk] for w in wgt3),
                                           jnp.zeros((32, CHUNK), F32)], axis=0).T.astype(BF16)
        keyf_ref[32 * c:32 * (c + 1), :] = key_rows(parts, 0)
        keyb_ref[32 * c:32 * (c + 1), :] = key_rows(parts, gh)

    def within(c, carry):
        t0 = pl.multiple_of(c * CHUNK, CHUNK)
        rows = pl.ds(t0, CHUNK)
        x = x_ref[0, rows, :]
        cm = c_ref[0, rows, :]
        bt = b_ref[0, rows, :].T
        bt_ref[rows, :] = bt
        dtt = dts_ref[:, rows]
        query = qry_ref[rows, :]
        keys = pl.ds(pl.multiple_of(c * 32, 32), 32)
        arg_f = jnp.dot(query, jnp.concatenate([pick_f, keyf_ref[keys, :], zero_tail], axis=0),
                        preferred_element_type=F32)
        arg_b = jnp.dot(query, jnp.concatenate([pick_b, keyb_ref[keys, :], zero_tail], axis=0),
                        preferred_element_type=F32)
        cb = _dot(cm, bt)
        ydiag = []
        for p in range(gh // 2):
            wms = []
            for h in (2 * p, 2 * p + 1):
                arg = jnp.where(si <= li, arg_f[:, CHUNK * h:CHUNK * (h + 1)], arg_b[:, CHUNK * h:CHUNK * (h + 1)])
                coef = jnp.where(si < li, dtt[h:h + 1, :],
                                 jnp.where(si > li, dtt[gh + h:gh + h + 1, :], dtt[h:h + 1, :] + dtt[gh + h:gh + h + 1, :]))
                wms.append(cb * jnp.exp(arg) * coef)
            xp = x[:, LANES * p:LANES * (p + 1)]
            rhs = jnp.concatenate([jnp.where(lane < hd, xp, 0.0), jnp.where(lane >= hd, xp, 0.0)], axis=0)
            ydiag.append(_dot(jnp.concatenate(wms, axis=1), rhs))
        ya_ref[rows, :] = jnp.concatenate(ydiag, axis=1)
        return carry

    lax.fori_loop(0, n_chunks, within, 0, unroll=2)

    for d, st in ((0, htf_ref), (1, htb_ref)):
        if has_h0:
            st[...] = h0_ref[0, d].reshape(hp, -1).T
        else:
            st[...] = jnp.zeros_like(st)

    def carried(c, st, spread, edge):
        rows = pl.ds(pl.multiple_of(c * CHUNK, CHUNK), CHUNK)
        ew = jnp.dot(col_ref[rows, :], spread, preferred_element_type=F32)
        ht = st[...]
        ya_ref[rows, :] += _dot(c_ref[0, rows, :], ht) * ew[:, :hp]
        st[...] = ht * ew[edge:edge + 1, :hp] + _dot(bt_ref[rows, :], x_ref[0, rows, :] * ew[:, hp:])

    def across(i, carry):
        carried(i, htf_ref, spread_f, CHUNK - 1)
        carried(n_chunks - 1 - i, htb_ref, spread_b, 0)
        return carry

    lax.fori_loop(0, n_chunks, across, 0)
    if emit_state:
        for d, st in ((0, htf_ref), (1, htb_ref)):
            st_ref[0, d] = st[...].T.reshape(st_ref.shape[2:])

    def finish(c, carry):
        rows = pl.ds(pl.multiple_of(c * CHUNK, CHUNK), CHUNK)
        y = (ya_ref[rows, :] + dskip_ref[...] * x_ref[0, rows, :]) * z_ref[0, rows, :]
        y_ref[0, rows, :] = _rms(y, nw_ref[...]).astype(BF16)
        return carry

    lax.fori_loop(0, n_chunks, finish, 0, unroll=2)


def _ssd_core(main, small, dt_bias, a_log, d_skip, norm_w, h0, *, batch, seq, emit_state):
    heads, groups, hd, hg = 64, 8, SSD_HEAD_DIM, SSD_GROUP_HEADS
    nstate = 128
    inner = heads * hd
    hp = inner // groups
    main = main.reshape(batch, seq, -1)
    dt_r = small.reshape(batch, seq, 2, groups, hg).transpose(0, 3, 2, 4, 1).reshape(batch, groups, 2 * hg, seq)

    def per_group(p):
        return p.astype(F32).reshape(2, groups, hg).transpose(1, 0, 2).reshape(groups, 2 * hg, 1)

    dsk = jnp.repeat(d_skip.astype(F32), hd).reshape(1, inner)
    nw = norm_w.astype(F32).reshape(1, inner)
    xo, bo, co = inner // hp, 2 * inner // nstate, (2 * inner + groups * nstate) // nstate
    in_specs = [
        pl.BlockSpec((1, seq, hp), lambda b, g: (b, 0, g)),
        pl.BlockSpec((1, seq, hp), lambda b, g: (b, 0, xo + g)),
        pl.BlockSpec((1, seq, nstate), lambda b, g: (b, 0, bo + g)),
        pl.BlockSpec((1, seq, nstate), lambda b, g: (b, 0, co + g)),
        pl.BlockSpec((1, 1, 2 * hg, seq), lambda b, g: (b, g, 0, 0)),
        pl.BlockSpec((1, 2 * hg, 1), lambda b, g: (g, 0, 0)),
        pl.BlockSpec((1, 2 * hg, 1), lambda b, g: (g, 0, 0)),
        pl.BlockSpec((1, hp), lambda b, g: (0, g)),
        pl.BlockSpec((1, hp), lambda b, g: (0, g)),
    ]
    args = [main, main, main, main, dt_r, per_group(dt_bias), per_group(a_log), dsk, nw]
    state_spec = pl.BlockSpec((1, 2, hg, hd, nstate), lambda b, g: (b, 0, g, 0, 0))
    if h0 is not None:
        in_specs.append(state_spec)
        args.append(h0.astype(F32))
    out_specs = [pl.BlockSpec((1, seq, hp), lambda b, g: (b, 0, g))]
    out_shape = [jax.ShapeDtypeStruct((batch, seq, inner), BF16)]
    if emit_state:
        out_specs.append(state_spec)
        out_shape.append(jax.ShapeDtypeStruct((batch, 2, heads, hd, nstate), F32))
    outs = pl.pallas_call(
        functools.partial(_ssd_kernel, seq=seq, has_h0=h0 is not None, emit_state=emit_state),
        grid=(batch, groups),
        in_specs=in_specs,
        out_specs=out_specs,
        out_shape=out_shape,
        scratch_shapes=[pltpu.VMEM((seq, nstate), F32), pltpu.VMEM((seq, hp), F32), pltpu.VMEM((nstate, hp), F32),
                        pltpu.VMEM((nstate, hp), F32), pltpu.VMEM((seq, LANES), BF16), pltpu.VMEM((seq, LANES), BF16),
                        pltpu.VMEM((seq // CHUNK * 32, hg * CHUNK), BF16), pltpu.VMEM((seq // CHUNK * 32, hg * CHUNK), BF16),
                        pltpu.VMEM((2 * hg, seq), F32)],
        compiler_params=_cparams("parallel", "parallel"),
        name="ssd_core",
    )(*args)
    return (outs[0], outs[1]) if emit_state else (outs[0], None)


def _mlstm_kernel(*refs, seq, has_state, emit_state):
    q_ref, k_ref, v_ref, o_ref, gr_ref, br_ref, nw_ref = refs[:7]
    rest = list(refs[7:])
    if has_state:
        c0_ref, n0_ref, m0_ref = rest.pop(0), rest.pop(0), rest.pop(0)
    y_ref = rest.pop(0)
    if emit_state:
        cf_ref, nf_ref, mf_ref = rest.pop(0), rest.pop(0), rest.pop(0)
    hf_ref, hb_ref, caf_ref, cab_ref = rest
    n_chunks = seq // CHUNK
    dqk = q_ref.shape[2]
    dv = v_ref.shape[2]
    li = lax.broadcasted_iota(jnp.int32, (CHUNK, CHUNK), 0)
    si = lax.broadcasted_iota(jnp.int32, (CHUNK, CHUNK), 1)
    lane = lax.broadcasted_iota(jnp.int32, (CHUNK, LANES), 1)
    ones_col = (lane == 0).astype(F32)
    k_scale = dqk ** -0.5

    def load_state(ca_ref, d):
        if has_state:
            ncol = jnp.where(lane[:dqk] == 0, n0_ref[0, d, 0], 0.0)
            ca_ref[...] = jnp.concatenate([c0_ref[0, d, 0], ncol], axis=1)
            return m0_ref[0, d, 0]
        ca_ref[...] = jnp.zeros_like(ca_ref)
        return jnp.zeros((1, 1), F32)

    def store_state(ca_ref, d, m):
        if emit_state:
            cf_ref[0, d, 0] = ca_ref[:, :dv]
            nf_ref[0, d, 0] = ca_ref[:, dv:dv + 1]
            mf_ref[0, d, 0] = m

    def gates(t0):
        g = gr_ref[0, 0, :, pl.ds(t0, CHUNK)] + br_ref[0]
        ri = lax.broadcasted_iota(jnp.int32, g.shape, 0)
        g = jnp.where((ri == 1) | (ri == 3), -_softplus(-g), g)
        cum = _dot_exact(g, _tri(CHUNK, False))
        cols = jnp.concatenate([cum, g, jnp.zeros((CHUNK - 16, CHUNK), F32)], axis=0).T
        return g, cum, cols

    def scores(qk, cum_col, neg_row, mask, m):
        dmat = jnp.where(mask, cum_col + neg_row, -jnp.inf)
        inter = cum_col + m
        mt = jnp.maximum(inter, jnp.max(dmat, axis=1, keepdims=True))
        return qk * jnp.exp(dmat - mt), jnp.exp(inter - mt), mt

    def readout(ca_ref, q, v, s):
        ca = ca_ref[...]
        return ca, _dot(q, ca), _dot(s, v)

    def normalise(qc, sv, s, w_in, mt, neg_row, tot, m, kt):
        num = sv + qc[:, :dv] * w_in
        den = jnp.sum(s, axis=1, keepdims=True) + qc[:, dv:dv + 1] * w_in
        hout = num / jnp.maximum(jnp.abs(den), jnp.exp(-mt))
        gl = tot + neg_row
        m_new = jnp.maximum(tot + m, jnp.max(gl, axis=1, keepdims=True))
        return hout, m_new, jnp.exp(tot + m - m_new), kt * jnp.exp(gl - m_new)

    def chain(c, backward, ca_ref, h_ref, m):
        t0 = pl.multiple_of(c * CHUNK, CHUNK)
        rows = pl.ds(t0, CHUNK)
        q = q_ref[0, rows, :]
        kt = (k_ref[0, rows, :] * k_scale).T
        v = v_ref[0, rows, :]
        qk = _dot(q, kt)
        g, cum, cols = gates(t0)
        if backward:
            tot = cum[3:4, CHUNK - 1:CHUNK]
            cum_col = tot - cols[:, 3:4] + cols[:, 11:12]
            neg = g[2:3] - (tot - cum[3:4] + g[3:4])
        else:
            tot = cum[1:2, CHUNK - 1:CHUNK]
            cum_col = cols[:, 1:2]
            neg = g[0:1] - cum[1:2]
        s, w_in, mt = scores(qk, cum_col, neg, si >= li if backward else si <= li, m)
        ca, qc, sv = readout(ca_ref, q, v, s)
        hout, m, dec, kw = normalise(qc, sv, s, w_in, mt, neg, tot, m, kt)
        ca_ref[...] = dec * ca + _dot(kw, jnp.concatenate([v, ones_col], axis=1))
        h_ref[rows, :] = hout
        return m

    def step(i, ms):
        return chain(i, False, caf_ref, hf_ref, ms[0]), chain(n_chunks - 1 - i, True, cab_ref, hb_ref, ms[1])

    m_f, m_b = lax.fori_loop(0, n_chunks, step, (load_state(caf_ref, 0), load_state(cab_ref, 1)))
    store_state(caf_ref, 0, m_f)
    store_state(cab_ref, 1, m_b)

    def finish(c, carry):
        rows = pl.ds(pl.multiple_of(c * CHUNK, CHUNK), CHUNK)
        h = _rms(hf_ref[rows, :] + hb_ref[rows, :], nw_ref[...])
        y_ref[0, rows, :] = (h * o_ref[0, rows, :]).astype(BF16)
        return carry

    lax.fori_loop(0, n_chunks, finish, 0)


def _mlstm_core(main, small, igate_b, fgate_b, norm_w, state, *, batch, seq, emit_state):
    heads, dqk, dv = 8, 128, 256
    main = main.reshape(batch, seq, -1)
    g = small[:, :4 * heads].reshape(batch, seq, 4, heads)
    gr = jnp.pad(g.transpose(0, 3, 2, 1), ((0, 0), (0, 0), (0, 4), (0, 0)))
    bias = jnp.stack([igate_b[0], fgate_b[0], igate_b[1], fgate_b[1]], axis=1).astype(F32)
    bias = jnp.pad(bias, ((0, 0), (0, 4)))
    in_specs = [
        pl.BlockSpec((1, seq, dqk), lambda b, h: (b, 0, h)),
        pl.BlockSpec((1, seq, dqk), lambda b, h: (b, 0, heads + h)),
        pl.BlockSpec((1, seq, dv), lambda b, h: (b, 0, heads + h)),
        pl.BlockSpec((1, seq, dv), lambda b, h: (b, 0, 2 * heads + h)),
        pl.BlockSpec((1, 1, 8, seq), lambda b, h: (b, h, 0, 0)),
        pl.BlockSpec((1, 8, 1), lambda b, h: (h, 0, 0)),
        pl.BlockSpec((1, dv), lambda b, h: (0, h)),
    ]
    args = [main, main, main, main, gr, bias.reshape(heads, 8, 1), norm_w.astype(F32).reshape(1, heads * dv)]
    c_spec = pl.BlockSpec((1, 2, 1, dqk, dv), lambda b, h: (b, 0, h, 0, 0))
    n_spec = pl.BlockSpec((1, 2, 1, dqk, 1), lambda b, h: (b, 0, h, 0, 0))
    m_spec = pl.BlockSpec((1, 2, 1, 1, 1), lambda b, h: (b, 0, h, 0, 0))
    if state is not None:
        c0, n0, m0 = state
        in_specs += [c_spec, n_spec, m_spec]
        args += [c0.astype(F32), n0.astype(F32).reshape(batch, 2, heads, dqk, 1),
                 m0.astype(F32).reshape(batch, 2, heads, 1, 1)]
    out_specs = [pl.BlockSpec((1, seq, dv), lambda b, h: (b, 0, h))]
    out_shape = [jax.ShapeDtypeStruct((batch, seq, heads * dv), BF16)]
    if emit_state:
        out_specs += [c_spec, n_spec, m_spec]
        out_shape += [jax.ShapeDtypeStruct((batch, 2, heads, dqk, dv), F32),
                      jax.ShapeDtypeStruct((batch, 2, heads, dqk, 1), F32),
                      jax.ShapeDtypeStruct((batch, 2, heads, 1, 1), F32)]
    outs = pl.pallas_call(
        functools.partial(_mlstm_kernel, seq=seq, has_state=state is not None, emit_state=emit_state),
        grid=(batch, heads),
        in_specs=in_specs,
        out_specs=out_specs,
        out_shape=out_shape,
        scratch_shapes=[pltpu.VMEM((seq, dv), F32), pltpu.VMEM((seq, dv), F32),
                        pltpu.VMEM((dqk, dv + LANES), F32), pltpu.VMEM((dqk, dv + LANES), F32)],
        compiler_params=_cparams("parallel", "parallel"),
        name="mlstm_core",
    )(*args)
    if emit_state:
        return outs[0], (outs[1], outs[2].reshape(batch, 2, heads, dqk), outs[3].reshape(batch, 2, heads))
    return outs[0], None


LRU_UNROLL = 8


def _scan8(a, b, reverse):
    row = lax.broadcasted_iota(jnp.int32, a.shape, 0)
    for k in (1, 2, 4):
        shift = SUBLANES - k if reverse else k
        valid = row < SUBLANES - k if reverse else row >= k
        a_prev = pltpu.roll(a, shift, 0)
        b_prev = pltpu.roll(b, shift, 0)
        b = jnp.where(valid, a * b_prev, 0.0) + b
        a = jnp.where(valid, a * a_prev, a)
    return a, b


def _lru_kernel(*refs, seq, rows, has_h0, emit_state):
    gate_ref, x_ref, wa_ref, wi_ref, ba_ref, bi_ref, lam_ref = refs[:7]
    rest = list(refs[7:])
    h0_ref = rest.pop(0) if has_h0 else None
    y_ref = rest.pop(0)
    st_ref = rest.pop(0) if emit_state else None
    hf_ref, hb_ref, af_ref, bf_ref, ab_ref, bb_ref = rest
    n_chunks = seq // rows
    n_groups = rows // SUBLANES
    width = x_ref.shape[2]

    def coefficients(c, d, a_ref, b_ref):
        xc = x_ref[0, pl.ds(pl.multiple_of(c * rows, rows), rows), :]
        r = _sigmoid(_dot(xc, wa_ref[d, 0]) + ba_ref[d])
        i = _sigmoid(_dot(xc, wi_ref[d, 0]) + bi_ref[d])
        log_a = -LRU_C * r * _softplus(-lam_ref[d])
        a = jnp.exp(log_a)
        a_ref[...] = a
        b_ref[...] = jnp.sqrt(-jnp.tanh(log_a) * (1.0 + a * a)) * (i * xc)

    def initial(d):
        return h0_ref[0, d] if has_h0 else jnp.zeros((1, width), F32)

    def step(i, carries):
        cf, cb = i, n_chunks - 1 - i
        coefficients(cf, 0, af_ref, bf_ref)
        coefficients(cb, 1, ab_ref, bb_ref)

        def group(gi, carries):
            carry_f, carry_b = carries
            r0 = pl.multiple_of(gi * SUBLANES, SUBLANES)
            a, b = _scan8(af_ref[pl.ds(r0, SUBLANES), :], bf_ref[pl.ds(r0, SUBLANES), :], False)
            h = b + a * carry_f
            hf_ref[pl.ds(pl.multiple_of(cf * rows + r0, SUBLANES), SUBLANES), :] = h
            carry_f = h[SUBLANES - 1:SUBLANES]
            r0 = pl.multiple_of((n_groups - 1 - gi) * SUBLANES, SUBLANES)
            a, b = _scan8(ab_ref[pl.ds(r0, SUBLANES), :], bb_ref[pl.ds(r0, SUBLANES), :], True)
            h = b + a * carry_b
            hb_ref[pl.ds(pl.multiple_of(cb * rows + r0, SUBLANES), SUBLANES), :] = h
            return carry_f, h[0:1]

        return lax.fori_loop(0, n_groups, group, carries, unroll=LRU_UNROLL)

    carry_f, carry_b = lax.fori_loop(0, n_chunks, step, (initial(0), initial(1)))
    if emit_state:
        st_ref[0, 0] = carry_f
        st_ref[0, 1] = carry_b

    def finish(c, carry):
        sl = pl.ds(pl.multiple_of(c * rows, rows), rows)
        y_ref[0, sl, :] = ((hf_ref[sl, :] + hb_ref[sl, :]) * gate_ref[0, sl, :]).astype(BF16)
        return carry

    lax.fori_loop(0, n_chunks, finish, 0)


def _lru_core(main, wa, ba, wi, bi, lam, h0, *, batch, seq, emit_state):
    nb, bs = wa.shape[1], wa.shape[2]
    width = nb * bs
    rows = min(seq, 256)
    main = main.reshape(batch, seq, -1)
    vec = lambda p: p.astype(F32).reshape(2, 1, width)
    in_specs = [
        pl.BlockSpec((1, seq, bs), lambda b, n: (b, 0, n)),
        pl.BlockSpec((1, seq, bs), lambda b, n: (b, 0, nb + n)),
        pl.BlockSpec((2, 1, bs, bs), lambda b, n: (0, n, 0, 0)),
        pl.BlockSpec((2, 1, bs, bs), lambda b, n: (0, n, 0, 0)),
        pl.BlockSpec((2, 1, bs), lambda b, n: (0, 0, n)),
        pl.BlockSpec((2, 1, bs), lambda b, n: (0, 0, n)),
        pl.BlockSpec((2, 1, bs), lambda b, n: (0, 0, n)),
    ]
    args = [main, main, wa, wi, vec(ba), vec(bi), vec(lam)]
    state_spec = pl.BlockSpec((1, 2, 1, bs), lambda b, n: (b, 0, 0, n))
    if h0 is not None:
        in_specs.append(state_spec)
        args.append(h0.astype(F32).reshape(batch, 2, 1, width))
    out_specs = [pl.BlockSpec((1, seq, bs), lambda b, n: (b, 0, n))]
    out_shape = [jax.ShapeDtypeStruct((batch, seq, width), BF16)]
    if emit_state:
        out_specs.append(state_spec)
        out_shape.append(jax.ShapeDtypeStruct((batch, 2, 1, width), F32))
    outs = pl.pallas_call(
        functools.partial(_lru_kernel, seq=seq, rows=rows, has_h0=h0 is not None, emit_state=emit_state),
        grid=(batch, nb),
        in_specs=in_specs,
        out_specs=out_specs,
        out_shape=out_shape,
        scratch_shapes=[pltpu.VMEM((seq, bs), F32), pltpu.VMEM((seq, bs), F32)]
                       + [pltpu.VMEM((rows, bs), F32) for _ in range(4)],
        compiler_params=_cparams("parallel", "parallel"),
        name="lru_core",
    )(*args)
    return (outs[0], outs[1].reshape(batch, 2, width)) if emit_state else (outs[0], None)


def _to_col_major(h):
    b, l, d = h.shape
    return h.reshape(b, l // GRID_W, GRID_W, d).transpose(0, 2, 1, 3).reshape(b, l, d)


def _to_row_major(h):
    b, l, d = h.shape
    return h.reshape(b, GRID_W, l // GRID_W, d).transpose(0, 2, 1, 3).reshape(b, l, d)


def _conv_over(n_plain, conv_w, conv_b):
    cw = jnp.concatenate([jnp.zeros((conv_w.shape[0], n_plain), F32), conv_w.astype(F32)], axis=1)
    cb = jnp.concatenate([jnp.zeros((n_plain,), F32), conv_b.astype(F32)])
    return cw, cb


def _trunk(x, mod, grid, states, p, w):
    batch, seq, _ = x.shape
    depth = mod.shape[0]
    emit = states is None
    tn = 512
    finals = dict(ssd=[], ml_c=[], ml_n=[], ml_m=[], lru=[])
    for l in range(depth):
        sh1, sc1, g1, sh2, sc2, g2 = (mod[l, :, k][:, None, :] for k in range(6))
        kind, j = l % N_MIXERS, l // N_MIXERS
        col = grid and j % 2 == 1
        xin = _to_col_major(x) if col else x
        if kind == 0:
            w_main = w["ssd_in_main"][j]
            inner = w["ssd_out"][j].shape[0]
            cw, cb = _conv_over(inner, p["ssd_conv_w"][j], p["ssd_conv_b"][j])
            kinds = ((inner // tn, "silu"), ((w_main.shape[1] - inner) // tn, "conv_silu"))
            main, small = _in_proj(xin, p["norm_mix_pre"][l], sc1, sh1, w_main, w["ssd_in_dt"][j],
                                   kinds=kinds, conv_w=cw, conv_b=cb, tn=tn)
            y, s = _ssd_core(main, small, p["ssd_dt_bias"][j], p["ssd_a_log"][j], p["ssd_d"][j], p["ssd_norm_w"][j],
                             None if emit else states["ssd"][:, j], batch=batch, seq=seq, emit_state=emit)
            finals["ssd"].append(s)
            w_out = w["ssd_out"][j]
        elif kind == 1:
            w_main = w["ml_in_main"][j]
            d_out = w["ml_out"][j].shape[0]
            kinds = (((w_main.shape[1] - d_out) // tn, "none"), (d_out // tn, "sigmoid"))
            main, small = _in_proj(xin, p["norm_mix_pre"][l], sc1, sh1, w_main, w["ml_in_gates"][j], kinds=kinds, tn=tn)
            st = None if emit else (states["ml_c"][:, j], states["ml_n"][:, j], states["ml_m"][:, j])
            y, s = _mlstm_core(main, small, p["ml_igate_b"][j], p["ml_fgate_b"][j], p["ml_norm_w"][j], st,
                               batch=batch, seq=seq, emit_state=emit)
            if emit:
                finals["ml_c"].append(s[0])
                finals["ml_n"].append(s[1])
                finals["ml_m"].append(s[2])
            w_out = w["ml_out"][j]
        else:
            w_main = w["lru_in"][j]
            width = w["lru_out"][j].shape[0]
            cw, cb = _conv_over(width, p["lru_conv_w"][j], p["lru_conv_b"][j])
            kinds = ((width // tn, "gelu"), (width // tn, "conv_none"))
            main, _ = _in_proj(xin, p["norm_mix_pre"][l], sc1, sh1, w_main, kinds=kinds, conv_w=cw, conv_b=cb, tn=tn)
            y, s = _lru_core(main, w["lru_wa"][j], p["lru_ba"][j], w["lru_wi"][j], p["lru_bi"][j], p["lru_lambda"][j],
                             None if emit else states["lru"][:, j], batch=batch, seq=seq, emit_state=emit)
            finals["lru"].append(s)
            w_out = w["lru_out"][j]
        xo = _out_proj(y, w_out, xin, g1, p["norm_mix_post"][l])
        x = _to_row_major(xo) if col else xo
        x = _mlp(x, p["norm_mlp_pre"][l], sc2, sh2, w["mlp_w1"][l], w["mlp_w2"][l], g2, p["norm_mlp_post"][l])
    if emit:
        return x, {k: jnp.stack(v, axis=1) for k, v in finals.items()}
    return x, None


def kernel(x_prompt, x_sample, state_ssd, state_mlstm_C, state_mlstm_n, state_mlstm_m, state_rglru, c, c_ctx, mod_w, mod_b, norm_mix_pre, norm_mix_post, norm_mlp_pre, norm_mlp_post, mlp_w1, mlp_w2, ssd_in_w, ssd_conv_w, ssd_conv_b, ssd_dt_bias, ssd_a_log, ssd_d, ssd_norm_w, ssd_out_w, ml_in_w, ml_igate_b, ml_fgate_b, ml_norm_w, ml_out_w, lru_in_w, lru_conv_w, lru_conv_b, lru_wa, lru_ba, lru_wi, lru_bi, lru_lambda, lru_out_w):
    p = dict(norm_mix_pre=norm_mix_pre, norm_mix_post=norm_mix_post, norm_mlp_pre=norm_mlp_pre,
             norm_mlp_post=norm_mlp_post, ssd_conv_w=ssd_conv_w, ssd_conv_b=ssd_conv_b, ssd_dt_bias=ssd_dt_bias,
             ssd_a_log=ssd_a_log, ssd_d=ssd_d, ssd_norm_w=ssd_norm_w, ml_igate_b=ml_igate_b, ml_fgate_b=ml_fgate_b,
             ml_norm_w=ml_norm_w, lru_conv_w=lru_conv_w, lru_conv_b=lru_conv_b, lru_ba=lru_ba, lru_bi=lru_bi,
             lru_lambda=lru_lambda)
    d_model = x_prompt.shape[-1]
    depth = mod_w.shape[0]
    ssd_main = ssd_in_w.shape[2] - LANES
    ml_main = ml_in_w.shape[2] - 4 * state_mlstm_m.shape[-1]
    bf = lambda a: a.astype(BF16)
    w = dict(
        mlp_w1=bf(mlp_w1), mlp_w2=bf(mlp_w2),
        ssd_in_main=bf(ssd_in_w[:, :, :ssd_main]), ssd_in_dt=bf(ssd_in_w[:, :, ssd_main:]), ssd_out=bf(ssd_out_w),
        ml_in_main=bf(ml_in_w[:, :, :ml_main]),
        ml_in_gates=bf(jnp.pad(ml_in_w[:, :, ml_main:], ((0, 0), (0, 0), (0, LANES - (ml_in_w.shape[2] - ml_main))))),
        ml_out=bf(ml_out_w), lru_in=bf(lru_in_w), lru_out=bf(lru_out_w), lru_wa=bf(lru_wa), lru_wi=bf(lru_wi),
    )
    n_dec = c.shape[0]
    n_rows = -(-(1 + n_dec) // SUBLANES) * SUBLANES
    cond = jnp.concatenate([c_ctx[None].astype(F32), c.astype(F32),
                            jnp.zeros((n_rows - 1 - n_dec, d_model), F32)], axis=0)
    mod = _mod_all(cond, mod_w, mod_b).reshape(depth, n_rows, 6, d_model)
    y_prompt, fin = _trunk(x_prompt, mod[:, 0:1], False, None, p, w)
    states = dict(ssd=state_ssd, ml_c=state_mlstm_C, ml_n=state_mlstm_n, ml_m=state_mlstm_m, lru=state_rglru)
    y_sample, _ = _trunk(x_sample, mod[:, 1:1 + n_dec], True, states, p, w)
    dt = x_prompt.dtype
    return (y_prompt, y_sample, fin["ssd"].astype(dt), fin["ml_c"].astype(dt), fin["ml_n"].astype(dt),
            fin["ml_m"].astype(dt), fin["lru"].astype(dt))
```

```python
import functools

import jax
import jax.numpy as jnp
from jax import lax
from jax.experimental import pallas as pl
from jax.experimental.pallas import tpu as pltpu

F32 = jnp.float32
BF16 = jnp.bfloat16
HIGHEST = lax.Precision.HIGHEST

EPS = 1e-6
CHUNK = 128
GRID_W = 64
N_MIXERS = 3
LRU_C = 8.0
SUBLANES = 8
LANES = 128
HALO = 16
VMEM_LIMIT_BYTES = 56 * 1024 * 1024


def _cparams(*semantics):
    return pltpu.CompilerParams(dimension_semantics=semantics, vmem_limit_bytes=VMEM_LIMIT_BYTES)


def _dot(a, b):
    return jnp.dot(a.astype(BF16), b.astype(BF16), preferred_element_type=F32)


def _dot_exact(a, b):
    return jnp.dot(a, b, precision=HIGHEST, preferred_element_type=F32)


def _softplus(x):
    return jnp.maximum(x, 0.0) + jnp.log1p(jnp.exp(-jnp.abs(x)))


def _sigmoid(x):
    return jax.nn.sigmoid(x)


def _silu(x):
    return x * jax.nn.sigmoid(x)


def _rms(x, w):
    return x * lax.rsqrt(jnp.mean(x * x, axis=-1, keepdims=True) + EPS) * w


def _tri(n, lower):
    r = lax.broadcasted_iota(jnp.int32, (n, n), 0)
    c = lax.broadcasted_iota(jnp.int32, (n, n), 1)
    return (r >= c if lower else r <= c).astype(F32)


def _split3(v):
    hi = v.astype(BF16).astype(F32)
    r = v - hi
    mid = r.astype(BF16).astype(F32)
    return hi, mid, r - mid


_ACTS = {
    "none": lambda v: v,
    "silu": _silu,
    "sigmoid": _sigmoid,
    "gelu": lambda v: jax.nn.gelu(v, approximate=True),
}


def _mod_kernel(c_ref, w_ref, b_ref, o_ref):
    a = _silu(c_ref[...])
    o_ref[0] = _dot(a, w_ref[0]) + b_ref[0]


def _mod_all(cond, mod_w, mod_b):
    depth, d, n = mod_w.shape
    r = cond.shape[0]
    tn = 1024
    return pl.pallas_call(
        _mod_kernel,
        grid=(depth, n // tn),
        in_specs=[
            pl.BlockSpec((r, d), lambda l, j: (0, 0)),
            pl.BlockSpec((1, d, tn), lambda l, j: (l, 0, j)),
            pl.BlockSpec((1, 1, tn), lambda l, j: (l, 0, j)),
        ],
        out_specs=pl.BlockSpec((1, r, tn), lambda l, j: (l, 0, j)),
        out_shape=jax.ShapeDtypeStruct((depth, r, n), F32),
        compiler_params=_cparams("parallel", "parallel"),
        name="mod_map",
    )(cond, mod_w, mod_b.reshape(depth, 1, n))


def _in_proj_kernel(*refs, kinds, seq, tm, halo, has_conv, has_small):
    refs = list(refs)
    x_ref = refs.pop(0)
    xp_ref, xn_ref = (refs.pop(0), refs.pop(0)) if halo else (None, None)
    nw_ref, sc_ref, sh_ref, w_ref = refs.pop(0), refs.pop(0), refs.pop(0), refs.pop(0)
    cw_ref, cb_ref = (refs.pop(0), refs.pop(0)) if has_conv else (None, None)
    ws_ref = refs.pop(0) if has_small else None
    o_ref = refs.pop(0)
    os_ref = refs.pop(0) if has_small else None
    (h_ref,) = refs
    i, j = pl.program_id(0), pl.program_id(1)
    pad = HALO if halo else 0
    main = pl.ds(pad, tm)

    @pl.when(j == 0)
    def _():
        norm = lambda v: (_rms(v, nw_ref[...]) * (1.0 + sc_ref[0]) + sh_ref[0]).astype(BF16)
        h_ref[main, :] = norm(x_ref[...])
        if halo:
            h_ref[pl.ds(0, pad), :] = norm(xp_ref[...])
            h_ref[pl.ds(pad + tm, pad), :] = norm(xn_ref[...])
        if has_small:
            os_ref[...] = lax.dot_general(ws_ref[...], h_ref[main, :], (((1,), (1,)), ((), ())),
                                          preferred_element_type=F32)

    def plain(act):
        o_ref[...] = act(jnp.dot(h_ref[main, :], w_ref[...], preferred_element_type=F32))

    def conv(act):
        r = jnp.dot(h_ref[...], w_ref[...], preferred_element_type=F32)
        n = r.shape[0]
        if halo:
            lead = jnp.where((i * tm) % seq == 0, 0.0, r[:pad])
            tail = jnp.where(((i + 1) * tm) % seq == 0, 0.0, r[pad + tm:])
            r = jnp.concatenate([lead, r[pad:pad + tm], tail], axis=0)
        tap = lambda shift: pltpu.roll(r, shift % n, 0)[pad:pad + tm]
        t2, t1, tn1 = tap(2), tap(1), tap(-1)
        if not halo:
            pos = lax.broadcasted_iota(jnp.int32, r.shape, 0) & (seq - 1)
            t2 = jnp.where(pos >= 2, t2, 0.0)
            t1 = jnp.where(pos >= 1, t1, 0.0)
            tn1 = jnp.where(pos < seq - 1, tn1, 0.0)
        w = cw_ref[...]
        o_ref[...] = act(w[0:1] * t2 + w[1:2] * t1 + w[2:3] * r[pad:pad + tm] + w[3:4] * tn1 + cb_ref[...])

    start = 0
    for count, kind in kinds:
        body = (functools.partial(conv, _ACTS[kind[5:]]) if kind.startswith("conv_")
                else functools.partial(plain, _ACTS[kind]))
        pl.when((j >= start) & (j < start + count))(body)
        start += count


def _in_proj(x, norm_w, scale, shift, w_main, w_small=None, *, kinds, conv_w=None, conv_b=None, tm=1024, tn=512):
    b, l, d = x.shape
    m = b * l
    n = w_main.shape[1]
    per_batch = scale.shape[0] > 1
    assert m % tm == 0 and n % tn == 0 and (not per_batch or l % tm == 0)
    assert sum(c for c, _ in kinds) == n // tn
    has_conv = any(k.startswith("conv_") for _, k in kinds)
    halo = has_conv and l > tm
    assert not has_conv or (l % tm == 0 if halo else (tm % l == 0 and l & (l - 1) == 0))
    mod_idx = (lambda i, j: ((i * tm) // l, 0, 0)) if per_batch else (lambda i, j: (0, 0, 0))
    has_small = w_small is not None
    x2 = x.reshape(m, d)
    in_specs = [pl.BlockSpec((tm, d), lambda i, j: (i, 0))]
    args = [x2]
    if halo:
        per_tile, last = tm // HALO, m // HALO - 1
        in_specs += [pl.BlockSpec((HALO, d), lambda i, j: (jnp.maximum(i * per_tile - 1, 0), 0)),
                     pl.BlockSpec((HALO, d), lambda i, j: (jnp.minimum((i + 1) * per_tile, last), 0))]
        args += [x2, x2]
    in_specs += [
        pl.BlockSpec((1, d), lambda i, j: (0, 0)),
        pl.BlockSpec((1, 1, d), mod_idx),
        pl.BlockSpec((1, 1, d), mod_idx),
        pl.BlockSpec((d, tn), lambda i, j: (0, j)),
    ]
    args += [norm_w.reshape(1, d), scale, shift, w_main]
    if has_conv:
        in_specs += [pl.BlockSpec((4, tn), lambda i, j: (0, j)), pl.BlockSpec((1, tn), lambda i, j: (0, j))]
        args += [conv_w.astype(F32), conv_b.astype(F32).reshape(1, n)]
    out_specs = [pl.BlockSpec((tm, tn), lambda i, j: (i, j))]
    out_shape = [jax.ShapeDtypeStruct((m, n), F32)]
    if has_small:
        ns = w_small.shape[0]
        in_specs.append(pl.BlockSpec((ns, d), lambda i, j: (0, 0)))
        out_specs.append(pl.BlockSpec((ns, tm), lambda i, j: (0, i)))
        out_shape.append(jax.ShapeDtypeStruct((ns, m), F32))
        args.append(w_small)
    outs = pl.pallas_call(
        functools.partial(_in_proj_kernel, kinds=tuple(kinds), seq=l, tm=tm, halo=halo, has_conv=has_conv,
                          has_small=has_small),
        grid=(m // tm, n // tn),
        in_specs=in_specs,
        out_specs=out_specs,
        out_shape=out_shape,
        scratch_shapes=[pltpu.VMEM((tm + (2 * HALO if halo else 0), d), BF16)],
        compiler_params=_cparams("parallel", "arbitrary"),
        name="in_proj",
    )(*args)
    return (outs[0], outs[1]) if has_small else (outs[0], None)


def _out_proj_kernel(y_ref, w_ref, x_ref, g_ref, nw_ref, o_ref, *acc, nk):
    k = pl.program_id(1)
    part = jnp.dot(y_ref[...], w_ref[...], preferred_element_type=F32)
    finish = lambda f: x_ref[...] + g_ref[0] * _rms(f, nw_ref[...])
    if nk == 1:
        o_ref[...] = finish(part)
        return
    (acc_ref,) = acc

    @pl.when(k == 0)
    def _():
        acc_ref[...] = part

    @pl.when((k > 0) & (k < nk - 1))
    def _():
        acc_ref[...] += part

    @pl.when(k == nk - 1)
    def _():
        o_ref[...] = finish(acc_ref[...] + part)


def _out_proj(y, w, x, gate, norm_w, *, tm=512, tk=2048):
    b, l, d = x.shape
    m = b * l
    kin = w.shape[0]
    per_batch = gate.shape[0] > 1
    assert m % tm == 0 and kin % tk == 0 and (not per_batch or l % tm == 0)
    mod_idx = (lambda i, k: ((i * tm) // l, 0, 0)) if per_batch else (lambda i, k: (0, 0, 0))
    nk = kin // tk
    out = pl.pallas_call(
        functools.partial(_out_proj_kernel, nk=nk),
        grid=(m // tm, nk),
        in_specs=[
            pl.BlockSpec((tm, tk), lambda i, k: (i, k)),
            pl.BlockSpec((tk, d), lambda i, k: (k, 0)),
            pl.BlockSpec((tm, d), lambda i, k: (i, 0)),
            pl.BlockSpec((1, 1, d), mod_idx),
            pl.BlockSpec((1, d), lambda i, k: (0, 0)),
        ],
        out_specs=pl.BlockSpec((tm, d), lambda i, k: (i, 0)),
        out_shape=jax.ShapeDtypeStruct((m, d), F32),
        scratch_shapes=[pltpu.VMEM((tm, d), F32)] if nk > 1 else [],
        compiler_params=_cparams("parallel", "arbitrary"),
        name="out_proj",
    )(y.reshape(m, kin), w, x.reshape(m, d), gate, norm_w.reshape(1, d))
    return out.reshape(b, l, d)


def _mlp_kernel(x_ref, nw_ref, sc_ref, sh_ref, w1_ref, w2_ref, g_ref, pw_ref, o_ref, h_ref, acc_ref, *, nf):
    j = pl.program_id(1)

    @pl.when(j == 0)
    def _():
        h = _rms(x_ref[...], nw_ref[...]) * (1.0 + sc_ref[0]) + sh_ref[0]
        h_ref[...] = h.astype(BF16)
        acc_ref[...] = jnp.zeros_like(acc_ref)

    t = jnp.dot(h_ref[...], w1_ref[...], preferred_element_type=F32)
    t = jnp.square(jnp.maximum(t, 0.0))
    acc_ref[...] += jnp.dot(t.astype(BF16), w2_ref[...], preferred_element_type=F32)

    @pl.when(j == nf - 1)
    def _():
        o_ref[...] = x_ref[...] + g_ref[0] * _rms(acc_ref[...], pw_ref[...])


def _mlp(x, pre_w, scale, shift, w1, w2, gate, post_w, *, tm=512, tf=512):
    b, l, d = x.shape
    m = b * l
    dff = w1.shape[1]
    per_batch = scale.shape[0] > 1
    assert m % tm == 0 and dff % tf == 0 and (not per_batch or l % tm == 0)
    mod_idx = (lambda i, j: ((i * tm) // l, 0, 0)) if per_batch else (lambda i, j: (0, 0, 0))
    nf = dff // tf
    out = pl.pallas_call(
        functools.partial(_mlp_kernel, nf=nf),
        grid=(m // tm, nf),
        in_specs=[
            pl.BlockSpec((tm, d), lambda i, j: (i, 0)),
            pl.BlockSpec((1, d), lambda i, j: (0, 0)),
            pl.BlockSpec((1, 1, d), mod_idx),
            pl.BlockSpec((1, 1, d), mod_idx),
            pl.BlockSpec((d, tf), lambda i, j: (0, j)),
            pl.BlockSpec((tf, d), lambda i, j: (j, 0)),
            pl.BlockSpec((1, 1, d), mod_idx),
            pl.BlockSpec((1, d), lambda i, j: (0, 0)),
        ],
        out_specs=pl.BlockSpec((tm, d), lambda i, j: (i, 0)),
        out_shape=jax.ShapeDtypeStruct((m, d), F32),
        scratch_shapes=[pltpu.VMEM((tm, d), BF16), pltpu.VMEM((tm, d), F32)],
        compiler_params=_cparams("parallel", "arbitrary"),
        name="mlp",
    )(x.reshape(m, d), pre_w.reshape(1, d), scale, shift, w1, w2, gate, post_w.reshape(1, d))
    return out.reshape(b, l, d)


SSD_GROUP_HEADS = 8
SSD_HEAD_DIM = 64


def _ssd_kernel(*refs, seq, has_h0, emit_state):
    z_ref, x_ref, b_ref, c_ref, dtf_ref, dtb_ref, biast_ref, alogt_ref, dskip_ref, nw_ref = refs[:10]
    rest = list(refs[10:])
    h0_ref = rest.pop(0) if has_h0 else None
    y_ref = rest.pop(0)
    st_ref = rest.pop(0) if emit_state else None
    bt_ref, ya_ref, htf_ref, htb_ref, col_ref, qry_ref, keyf_ref, keyb_ref, dts_ref = rest
    n_chunks = seq // CHUNK
    gh, hd = SSD_GROUP_HEADS, SSD_HEAD_DIM
    hp = gh * hd
    wide = gh * CHUNK
    li = lax.broadcasted_iota(jnp.int32, (CHUNK, CHUNK), 0)
    si = lax.broadcasted_iota(jnp.int32, (CHUNK, CHUNK), 1)
    lane = lax.broadcasted_iota(jnp.int32, (CHUNK, LANES), 1)

    onehot = lambda cond: jnp.where(cond, 1.0, 0.0).astype(BF16)
    k_i = lax.broadcasted_iota(jnp.int32, (48, wide), 0)
    head_of_lane = lax.broadcasted_iota(jnp.int32, (48, wide), 1) >> 7
    pick_f = onehot((k_i & 15) == head_of_lane)
    pick_b = onehot((k_i & 15) == gh + head_of_lane)
    zero_tail = jnp.zeros((48, wide), BF16)
    diag_blocks = (lax.broadcasted_iota(jnp.int32, (gh, wide), 0)
                   == lax.broadcasted_iota(jnp.int32, (gh, wide), 1) >> 7)
    e_k = lax.broadcasted_iota(jnp.int32, (CHUNK, 2 * hp), 0)
    e_j = lax.broadcasted_iota(jnp.int32, (CHUNK, 2 * hp), 1)
    e_head = (e_j & (hp - 1)) >> 6
    in_rows = ((e_j < hp) & (e_k < 48)) | ((e_j >= hp) & (e_k >= 48) & (e_k < 96))
    spread = lambda d: onehot(in_rows & ((e_k & 15) == d + e_head))
    spread_f, spread_b = spread(0), spread(gh)

    def key_rows(parts, d):
        blocks = [jnp.where(diag_blocks, -jnp.tile(p[d:d + gh], (1, gh)), 0.0) for p in parts]
        blocks.append(jnp.zeros((gh, wide), F32))
        return jnp.concatenate(blocks, axis=0).astype(BF16)

    dt_all = _softplus(jnp.concatenate([dtf_ref[...], dtb_ref[...]], axis=0) + biast_ref[0])
    dts_ref[...] = dt_all
    stack = lambda v: jnp.concatenate([v[:, CHUNK * c:CHUNK * (c + 1)] for c in range(n_chunks)], axis=0)
    dt_st = stack(dt_all)
    a_st = stack(dt_all * (-jnp.exp(alogt_ref[0])))
    cs = _dot_exact(a_st, _tri(CHUNK, False))
    tot = cs[:, CHUNK - 1:CHUNK]
    backward = (lax.broadcasted_iota(jnp.int32, cs.shape, 0) & (2 * gh - 1)) >= gh
    pos = jnp.where(backward, tot - cs + a_st, cs)
    pos3, exp3, wgt3 = _split3(pos), _split3(jnp.exp(pos)), _split3(dt_st * jnp.exp(tot - pos))
    for c in range(n_chunks):
        blk, tok = slice(2 * gh * c, 2 * gh * (c + 1)), slice(CHUNK * c, CHUNK * (c + 1))
        parts = [p[blk] for p in pos3]
        qry_ref[tok, :] = jnp.concatenate([*parts, jnp.ones((32, CHUNK), F32), jnp.zeros((48, CHUNK), F32)],
                                          axis=0).T.astype(BF16)
        col_ref[tok, :] = jnp.concatenate([*(e[blk] for e in exp3), *(w[blk] for w in wgt3),
                                           jnp.zeros((32, CHUNK), F32)], axis=0).T.astype(BF16)
        keyf_ref[32 * c:32 * (c + 1), :] = key_rows(parts, 0)
        keyb_ref[32 * c:32 * (c + 1), :] = key_rows(parts, gh)

    def within(c, carry):
        t0 = pl.multiple_of(c * CHUNK, CHUNK)
        rows = pl.ds(t0, CHUNK)
        x = x_ref[0, rows, :]
        cm = c_ref[0, rows, :]
        bt = b_ref[0, rows, :].T
        bt_ref[rows, :] = bt
        dtt = dts_ref[:, rows]
        query = qry_ref[rows, :]
        keys = pl.ds(pl.multiple_of(c * 32, 32), 32)
        arg_f = jnp.dot(query, jnp.concatenate([pick_f, keyf_ref[keys, :], zero_tail], axis=0),
                        preferred_element_type=F32)
        arg_b = jnp.dot(query, jnp.concatenate([pick_b, keyb_ref[keys, :], zero_tail], axis=0),
                        preferred_element_type=F32)
        cb = _dot(cm, bt)
        ydiag = []
        for p in range(gh // 2):
            wms = []
            for h in (2 * p, 2 * p + 1):
                arg = jnp.where(si <= li, arg_f[:, CHUNK * h:CHUNK * (h + 1)], arg_b[:, CHUNK * h:CHUNK * (h + 1)])
                coef = jnp.where(si < li, dtt[h:h + 1, :],
                                 jnp.where(si > li, dtt[gh + h:gh + h + 1, :], dtt[h:h + 1, :] + dtt[gh + h:gh + h + 1, :]))
                wms.append(cb * jnp.exp(arg) * coef)
            xp = x[:, LANES * p:LANES * (p + 1)]
            rhs = jnp.concatenate([jnp.where(lane < hd, xp, 0.0), jnp.where(lane >= hd, xp, 0.0)], axis=0)
            ydiag.append(_dot(jnp.concatenate(wms, axis=1), rhs))
        ya_ref[rows, :] = jnp.concatenate(ydiag, axis=1)
        return carry

    lax.fori_loop(0, n_chunks, within, 0, unroll=min(4, n_chunks))

    for d, st in ((0, htf_ref), (1, htb_ref)):
        if has_h0:
            st[...] = h0_ref[0, d].reshape(hp, -1).T
        else:
            st[...] = jnp.zeros_like(st)

    def carried(c, st, spread, edge):
        rows = pl.ds(pl.multiple_of(c * CHUNK, CHUNK), CHUNK)
        ew = jnp.dot(col_ref[rows, :], spread, preferred_element_type=F32)
        ht = st[...]
        ya_ref[rows, :] += _dot(c_ref[0, rows, :], ht) * ew[:, :hp]
        st[...] = ht * ew[edge:edge + 1, :hp] + _dot(bt_ref[rows, :], x_ref[0, rows, :] * ew[:, hp:])

    def across(i, carry):
        carried(i, htf_ref, spread_f, CHUNK - 1)
        carried(n_chunks - 1 - i, htb_ref, spread_b, 0)
        return carry

    lax.fori_loop(0, n_chunks, across, 0)
    if emit_state:
        for d, st in ((0, htf_ref), (1, htb_ref)):
            st_ref[0, d] = st[...].T.reshape(st_ref.shape[2:])

    def finish(c, carry):
        rows = pl.ds(pl.multiple_of(c * CHUNK, CHUNK), CHUNK)
        y = (ya_ref[rows, :] + dskip_ref[...] * x_ref[0, rows, :]) * z_ref[0, rows, :]
        y_ref[0, rows, :] = _rms(y, nw_ref[...]).astype(BF16)
        return carry

    lax.fori_loop(0, n_chunks, finish, 0, unroll=2)


def _ssd_core(main, small, dt_bias, a_log, d_skip, norm_w, h0, *, batch, seq, emit_state):
    heads, groups, hd, hg = 64, 8, SSD_HEAD_DIM, SSD_GROUP_HEADS
    nstate = 128
    inner = heads * hd
    hp = inner // groups
    main = main.reshape(batch, seq, -1)

    def per_group(p):
        return p.astype(F32).reshape(2, groups, hg).transpose(1, 0, 2).reshape(groups, 2 * hg, 1)

    dsk = jnp.repeat(d_skip.astype(F32), hd).reshape(1, inner)
    nw = norm_w.astype(F32).reshape(1, inner)
    xo, bo, co = inner // hp, 2 * inner // nstate, (2 * inner + groups * nstate) // nstate
    in_specs = [
        pl.BlockSpec((1, seq, hp), lambda b, g: (b, 0, g)),
        pl.BlockSpec((1, seq, hp), lambda b, g: (b, 0, xo + g)),
        pl.BlockSpec((1, seq, nstate), lambda b, g: (b, 0, bo + g)),
        pl.BlockSpec((1, seq, nstate), lambda b, g: (b, 0, co + g)),
        pl.BlockSpec((hg, seq), lambda b, g: (g, b)),
        pl.BlockSpec((hg, seq), lambda b, g: (groups + g, b)),
        pl.BlockSpec((1, 2 * hg, 1), lambda b, g: (g, 0, 0)),
        pl.BlockSpec((1, 2 * hg, 1), lambda b, g: (g, 0, 0)),
        pl.BlockSpec((1, hp), lambda b, g: (0, g)),
        pl.BlockSpec((1, hp), lambda b, g: (0, g)),
    ]
    args = [main, main, main, main, small, small, per_group(dt_bias), per_group(a_log), dsk, nw]
    state_spec = pl.BlockSpec((1, 2, hg, hd, nstate), lambda b, g: (b, 0, g, 0, 0))
    if h0 is not None:
        in_specs.append(state_spec)
        args.append(h0.astype(F32))
    out_specs = [pl.BlockSpec((1, seq, hp), lambda b, g: (b, 0, g))]
    out_shape = [jax.ShapeDtypeStruct((batch, seq, inner), BF16)]
    if emit_state:
        out_specs.append(state_spec)
        out_shape.append(jax.ShapeDtypeStruct((batch, 2, heads, hd, nstate), F32))
    outs = pl.pallas_call(
        functools.partial(_ssd_kernel, seq=seq, has_h0=h0 is not None, emit_state=emit_state),
        grid=(batch, groups),
        in_specs=in_specs,
        out_specs=out_specs,
        out_shape=out_shape,
        scratch_shapes=[pltpu.VMEM((seq, nstate), F32), pltpu.VMEM((seq, hp), F32), pltpu.VMEM((nstate, hp), F32),
                        pltpu.VMEM((nstate, hp), F32), pltpu.VMEM((seq, LANES), BF16), pltpu.VMEM((seq, LANES), BF16),
                        pltpu.VMEM((seq // CHUNK * 32, hg * CHUNK), BF16), pltpu.VMEM((seq // CHUNK * 32, hg * CHUNK), BF16),
                        pltpu.VMEM((2 * hg, seq), F32)],
        compiler_params=_cparams("parallel", "parallel"),
        name="ssd_core",
    )(*args)
    return (outs[0], outs[1]) if emit_state else (outs[0], None)


def _mlstm_kernel(*refs, seq, has_state, emit_state):
    q_ref, k_ref, v_ref, o_ref, gr_ref, br_ref, nw_ref = refs[:7]
    rest = list(refs[7:])
    if has_state:
        c0_ref, n0_ref, m0_ref = rest.pop(0), rest.pop(0), rest.pop(0)
    y_ref = rest.pop(0)
    if emit_state:
        cf_ref, nf_ref, mf_ref = rest.pop(0), rest.pop(0), rest.pop(0)
    sv_ref, rsum_ref, cmxb_ref, accb_ref, u_ref, rows_ref, cmx_ref, cols_ref, caf_ref, cab_ref = rest
    n_chunks = seq // CHUNK
    dqk = q_ref.shape[2]
    dv = v_ref.shape[2]
    li = lax.broadcasted_iota(jnp.int32, (CHUNK, CHUNK), 0)
    si = lax.broadcasted_iota(jnp.int32, (CHUNK, CHUNK), 1)
    ones_blk = jnp.ones((CHUNK, LANES), F32)
    k_scale = dqk ** -0.5

    def load_state(ca_ref, d):
        if has_state:
            ca_ref[...] = jnp.concatenate([c0_ref[0, d, 0], jnp.broadcast_to(n0_ref[0, d, 0], (dqk, LANES))], axis=1)
            return m0_ref[0, d, 0]
        ca_ref[...] = jnp.zeros_like(ca_ref)
        return jnp.zeros((1, 1), F32)

    def store_state(ca_ref, d, m):
        if emit_state:
            cf_ref[0, d, 0] = ca_ref[:, :dv]
            nf_ref[0, d, 0] = ca_ref[:, dv:dv + 1]
            mf_ref[0, d, 0] = m


    g_all = gr_ref[...] + br_ref[0]
    g_st = jnp.concatenate([g_all[:, CHUNK * c:CHUNK * (c + 1)] for c in range(n_chunks)], axis=0)
    r8 = lax.broadcasted_iota(jnp.int32, g_st.shape, 0) & 7
    tok = lax.broadcasted_iota(jnp.int32, g_st.shape, 1)
    g_st = jnp.where((r8 == 1) | (r8 == 3), -_softplus(-g_st), g_st)
    cum = _dot_exact(g_st, _tri(CHUNK, False))
    tot = cum[:, CHUNK - 1:CHUNK]
    acc = jnp.where(r8 == 1, cum, jnp.where(r8 == 3, tot - cum + g_st, 0.0))
    neg = g_st - pltpu.roll(acc, acc.shape[0] - 1, 0)
    pmax, smax = neg, neg
    for k in (1, 2, 4, 8, 16, 32, 64):
        pmax = jnp.where(tok >= k, jnp.maximum(pmax, pltpu.roll(pmax, k, 1)), pmax)
        smax = jnp.where(tok < CHUNK - k, jnp.maximum(smax, pltpu.roll(smax, CHUNK - k, 1)), smax)
    cmx = jnp.where(r8 == 0, pmax, jnp.where(r8 == 2, smax, 0.0))
    terms_all = jnp.where((r8 == 0) | (r8 == 2), neg, acc)
    rows_ref[...] = terms_all
    cmx_ref[...] = cmx
    for c in range(n_chunks):
        blk = slice(8 * c, 8 * (c + 1))
        cols_ref[CHUNK * c:CHUNK * (c + 1), :] = jnp.concatenate(
            [terms_all[blk], cmx[blk], jnp.zeros((CHUNK - 16, CHUNK), F32)], axis=0).T

    def within(c, carry):
        rows = pl.ds(pl.multiple_of(c * CHUNK, CHUNK), CHUNK)
        r8c = pl.ds(pl.multiple_of(c * 8, 8), 8)
        q = q_ref[0, rows, :]
        kt = (k_ref[0, rows, :] * k_scale).T
        v = v_ref[0, rows, :]
        vaug = jnp.concatenate([v, ones_blk], axis=1)
        qk = _dot(q, kt)
        terms, cmxr, cols = rows_ref[r8c, :], cmx_ref[r8c, :], cols_ref[rows, :]
        for d in (0, 1):
            neg_row = terms[2 * d:2 * d + 1]
            cmx_rep = jnp.broadcast_to(cols[:, 8 + 2 * d:9 + 2 * d], (CHUNK, LANES))
            cmxb_ref[d, rows, :] = cmx_rep
            accb_ref[d, rows, :] = jnp.broadcast_to(cols[:, 1 + 2 * d:2 + 2 * d], (CHUNK, LANES))
            s0 = jnp.where(si >= li if d else si <= li, qk * jnp.exp(neg_row - cmx_rep), 0.0)
            sv_ref[d, rows, :] = _dot(s0, v)
            rsum_ref[d, rows, :] = jnp.broadcast_to(jnp.sum(s0, axis=1, keepdims=True), (CHUNK, LANES))
            maxneg = cmxr[2:3, 0:1] if d else cmxr[0:1, CHUNK - 1:CHUNK]
            u_ref[d, pl.ds(pl.multiple_of(c * dqk, dqk), dqk), :] = _dot(kt * jnp.exp(neg_row - maxneg), vaug)
        return carry

    lax.fori_loop(0, n_chunks, within, 0, unroll=2)

    def carried(c, d, ca_ref, m):
        rows = pl.ds(pl.multiple_of(c * CHUNK, CHUNK), CHUNK)
        r8c = pl.ds(pl.multiple_of(c * 8, 8), 8)
        ca = ca_ref[...]
        qc = _dot(q_ref[0, rows, :], ca)
        cmx_rep = cmxb_ref[d, rows, :]
        big = jnp.maximum(m, cmx_rep)
        r = jnp.exp(cmx_rep - big)
        w_in = jnp.exp(m - big)
        den = r * rsum_ref[d, rows, :] + w_in * qc[:, dv:]
        inv = 1.0 / jnp.maximum(jnp.abs(den), jnp.exp(-(accb_ref[d, rows, :] + big)))
        wide = lambda t: jnp.concatenate([t] * (dv // LANES), axis=1)
        sv_ref[d, rows, :] = wide(r * inv) * sv_ref[d, rows, :] + wide(w_in * inv) * qc[:, :dv]
        terms, cmxr = rows_ref[r8c, :], cmx_ref[r8c, :]
        tot_c = terms[3:4, 0:1] if d else terms[1:2, CHUNK - 1:CHUNK]
        maxneg = cmxr[2:3, 0:1] if d else cmxr[0:1, CHUNK - 1:CHUNK]
        top = jnp.maximum(m, maxneg)
        ca_ref[...] = jnp.exp(m - top) * ca + jnp.exp(maxneg - top) * u_ref[d, pl.ds(pl.multiple_of(c * dqk, dqk), dqk), :]
        return tot_c + top

    def across(i, ms):
        return carried(i, 0, caf_ref, ms[0]), carried(n_chunks - 1 - i, 1, cab_ref, ms[1])

    m_f, m_b = lax.fori_loop(0, n_chunks, across, (load_state(caf_ref, 0), load_state(cab_ref, 1)))
    store_state(caf_ref, 0, m_f)
    store_state(cab_ref, 1, m_b)

    def finish(c, carry):
        rows = pl.ds(pl.multiple_of(c * CHUNK, CHUNK), CHUNK)
        h = _rms(sv_ref[0, rows, :] + sv_ref[1, rows, :], nw_ref[...])
        y_ref[0, rows, :] = (h * o_ref[0, rows, :]).astype(BF16)
        return carry

    lax.fori_loop(0, n_chunks, finish, 0)


def _mlstm_core(main, small, igate_b, fgate_b, norm_w, state, *, batch, seq, emit_state):
    heads, dqk, dv = 8, 128, 256
    main = main.reshape(batch, seq, -1)
    bias = jnp.stack([igate_b[0], fgate_b[0], igate_b[1], fgate_b[1]], axis=1).astype(F32)
    bias = jnp.pad(bias, ((0, 0), (0, 4)))
    in_specs = [
        pl.BlockSpec((1, seq, dqk), lambda b, h: (b, 0, h)),
        pl.BlockSpec((1, seq, dqk), lambda b, h: (b, 0, heads + h)),
        pl.BlockSpec((1, seq, dv), lambda b, h: (b, 0, heads + h)),
        pl.BlockSpec((1, seq, dv), lambda b, h: (b, 0, 2 * heads + h)),
        pl.BlockSpec((8, seq), lambda b, h: (h, b)),
        pl.BlockSpec((1, 8, 1), lambda b, h: (h, 0, 0)),
        pl.BlockSpec((1, dv), lambda b, h: (0, h)),
    ]
    args = [main, main, main, main, small, bias.reshape(heads, 8, 1), norm_w.astype(F32).reshape(1, heads * dv)]
    c_spec = pl.BlockSpec((1, 2, 1, dqk, dv), lambda b, h: (b, 0, h, 0, 0))
    n_spec = pl.BlockSpec((1, 2, 1, dqk, 1), lambda b, h: (b, 0, h, 0, 0))
    m_spec = pl.BlockSpec((1, 2, 1, 1, 1), lambda b, h: (b, 0, h, 0, 0))
    if state is not None:
        c0, n0, m0 = state
        in_specs += [c_spec, n_spec, m_spec]
        args += [c0.astype(F32), n0.astype(F32).reshape(batch, 2, heads, dqk, 1),
                 m0.astype(F32).reshape(batch, 2, heads, 1, 1)]
    out_specs = [pl.BlockSpec((1, seq, dv), lambda b, h: (b, 0, h))]
    out_shape = [jax.ShapeDtypeStruct((batch, seq, heads * dv), BF16)]
    if emit_state:
        out_specs += [c_spec, n_spec, m_spec]
        out_shape += [jax.ShapeDtypeStruct((batch, 2, heads, dqk, dv), F32),
                      jax.ShapeDtypeStruct((batch, 2, heads, dqk, 1), F32),
                      jax.ShapeDtypeStruct((batch, 2, heads, 1, 1), F32)]
    outs = pl.pallas_call(
        functools.partial(_mlstm_kernel, seq=seq, has_state=state is not None, emit_state=emit_state),
        grid=(batch, heads),
        in_specs=in_specs,
        out_specs=out_specs,
        out_shape=out_shape,
        scratch_shapes=[pltpu.VMEM((2, seq, dv), F32), pltpu.VMEM((2, seq, LANES), F32),
                        pltpu.VMEM((2, seq, LANES), F32), pltpu.VMEM((2, seq, LANES), F32),
                        pltpu.VMEM((2, seq // CHUNK * dqk, dv + LANES), F32),
                        pltpu.VMEM((seq // CHUNK * 8, CHUNK), F32), pltpu.VMEM((seq // CHUNK * 8, CHUNK), F32),
                        pltpu.VMEM((seq, LANES), F32),
                        pltpu.VMEM((dqk, dv + LANES), F32), pltpu.VMEM((dqk, dv + LANES), F32)],
        compiler_params=_cparams("parallel", "parallel"),
        name="mlstm_core",
    )(*args)
    if emit_state:
        return outs[0], (outs[1], outs[2].reshape(batch, 2, heads, dqk), outs[3].reshape(batch, 2, heads))
    return outs[0], None


LRU_UNROLL = 8


def _scan8(a, b, reverse):
    row = lax.broadcasted_iota(jnp.int32, a.shape, 0)
    for k in (1, 2, 4):
        shift = SUBLANES - k if reverse else k
        valid = row < SUBLANES - k if reverse else row >= k
        a_prev = pltpu.roll(a, shift, 0)
        b_prev = pltpu.roll(b, shift, 0)
        b = jnp.where(valid, a * b_prev, 0.0) + b
        a = jnp.where(valid, a * a_prev, a)
    return a, b


def _lru_kernel(*refs, seq, rows, has_h0, emit_state):
    gate_ref, x_ref, wa_ref, wi_ref, ba_ref, bi_ref, lam_ref = refs[:7]
    rest = list(refs[7:])
    h0_ref = rest.pop(0) if has_h0 else None
    y_ref = rest.pop(0)
    st_ref = rest.pop(0) if emit_state else None
    hf_ref, hb_ref, af_ref, bf_ref, ab_ref, bb_ref = rest
    n_chunks = seq // rows
    n_groups = rows // SUBLANES
    width = x_ref.shape[2]

    def coefficients(c, d, a_ref, b_ref):
        xc = x_ref[0, pl.ds(pl.multiple_of(c * rows, rows), rows), :]
        r = _sigmoid(_dot(xc, wa_ref[d, 0]) + ba_ref[d])
        i = _sigmoid(_dot(xc, wi_ref[d, 0]) + bi_ref[d])
        log_a = -LRU_C * r * _softplus(-lam_ref[d])
        a = jnp.exp(log_a)
        a_ref[...] = a
        b_ref[...] = jnp.sqrt(-jnp.tanh(log_a) * (1.0 + a * a)) * (i * xc)

    def initial(d):
        return h0_ref[0, d] if has_h0 else jnp.zeros((1, width), F32)

    def step(i, carries):
        cf, cb = i, n_chunks - 1 - i
        coefficients(cf, 0, af_ref, bf_ref)
        coefficients(cb, 1, ab_ref, bb_ref)

        def group(gi, carries):
            carry_f, carry_b = carries
            r0 = pl.multiple_of(gi * SUBLANES, SUBLANES)
            a, b = _scan8(af_ref[pl.ds(r0, SUBLANES), :], bf_ref[pl.ds(r0, SUBLANES), :], False)
            h = b + a * carry_f
            hf_ref[pl.ds(pl.multiple_of(cf * rows + r0, SUBLANES), SUBLANES), :] = h
            carry_f = h[SUBLANES - 1:SUBLANES]
            r0 = pl.multiple_of((n_groups - 1 - gi) * SUBLANES, SUBLANES)
            a, b = _scan8(ab_ref[pl.ds(r0, SUBLANES), :], bb_ref[pl.ds(r0, SUBLANES), :], True)
            h = b + a * carry_b
            hb_ref[pl.ds(pl.multiple_of(cb * rows + r0, SUBLANES), SUBLANES), :] = h
            return carry_f, h[0:1]

        return lax.fori_loop(0, n_groups, group, carries, unroll=LRU_UNROLL)

    carry_f, carry_b = lax.fori_loop(0, n_chunks, step, (initial(0), initial(1)))
    if emit_state:
        st_ref[0, 0] = carry_f
        st_ref[0, 1] = carry_b

    def finish(c, carry):
        sl = pl.ds(pl.multiple_of(c * rows, rows), rows)
        y_ref[0, sl, :] = ((hf_ref[sl, :] + hb_ref[sl, :]) * gate_ref[0, sl, :]).astype(BF16)
        return carry

    lax.fori_loop(0, n_chunks, finish, 0)


def _lru_core(main, wa, ba, wi, bi, lam, h0, *, batch, seq, emit_state):
    nb, bs = wa.shape[1], wa.shape[2]
    width = nb * bs
    rows = min(seq, 256)
    main = main.reshape(batch, seq, -1)
    vec = lambda p: p.astype(F32).reshape(2, 1, width)
    in_specs = [
        pl.BlockSpec((1, seq, bs), lambda b, n: (b, 0, n)),
        pl.BlockSpec((1, seq, bs), lambda b, n: (b, 0, nb + n)),
        pl.BlockSpec((2, 1, bs, bs), lambda b, n: (0, n, 0, 0)),
        pl.BlockSpec((2, 1, bs, bs), lambda b, n: (0, n, 0, 0)),
        pl.BlockSpec((2, 1, bs), lambda b, n: (0, 0, n)),
        pl.BlockSpec((2, 1, bs), lambda b, n: (0, 0, n)),
        pl.BlockSpec((2, 1, bs), lambda b, n: (0, 0, n)),
    ]
    args = [main, main, wa, wi, vec(ba), vec(bi), vec(lam)]
    state_spec = pl.BlockSpec((1, 2, 1, bs), lambda b, n: (b, 0, 0, n))
    if h0 is not None:
        in_specs.append(state_spec)
        args.append(h0.astype(F32).reshape(batch, 2, 1, width))
    out_specs = [pl.BlockSpec((1, seq, bs), lambda b, n: (b, 0, n))]
    out_shape = [jax.ShapeDtypeStruct((batch, seq, width), BF16)]
    if emit_state:
        out_specs.append(state_spec)
        out_shape.append(jax.ShapeDtypeStruct((batch, 2, 1, width), F32))
    outs = pl.pallas_call(
        functools.partial(_lru_kernel, seq=seq, rows=rows, has_h0=h0 is not None, emit_state=emit_state),
        grid=(batch, nb),
        in_specs=in_specs,
        out_specs=out_specs,
        out_shape=out_shape,
        scratch_shapes=[pltpu.VMEM((seq, bs), F32), pltpu.VMEM((seq, bs), F32)]
                       + [pltpu.VMEM((rows, bs), F32) for _ in range(4)],
        compiler_params=_cparams("parallel", "parallel"),
        name="lru_core",
    )(*args)
    return (outs[0], outs[1].reshape(batch, 2, width)) if emit_state else (outs[0], None)


def _to_col_major(h):
    b, l, d = h.shape
    return h.reshape(b, l // GRID_W, GRID_W, d).transpose(0, 2, 1, 3).reshape(b, l, d)


def _to_row_major(h):
    b, l, d = h.shape
    return h.reshape(b, GRID_W, l // GRID_W, d).transpose(0, 2, 1, 3).reshape(b, l, d)


def _conv_over(n_plain, conv_w, conv_b):
    cw = jnp.concatenate([jnp.zeros((conv_w.shape[0], n_plain), F32), conv_w.astype(F32)], axis=1)
    cb = jnp.concatenate([jnp.zeros((n_plain,), F32), conv_b.astype(F32)])
    return cw, cb


def _trunk(x, mod, grid, states, p, w):
    batch, seq, _ = x.shape
    depth = mod.shape[0]
    emit = states is None
    tn = 512
    finals = dict(ssd=[], ml_c=[], ml_n=[], ml_m=[], lru=[])
    for l in range(depth):
        sh1, sc1, g1, sh2, sc2, g2 = (mod[l, :, k][:, None, :] for k in range(6))
        kind, j = l % N_MIXERS, l // N_MIXERS
        col = grid and j % 2 == 1
        xin = _to_col_major(x) if col else x
        if kind == 0:
            w_main = w["ssd_in_main"][j]
            inner = w["ssd_out"][j].shape[0]
            cw, cb = _conv_over(inner, p["ssd_conv_w"][j], p["ssd_conv_b"][j])
            kinds = ((inner // tn, "silu"), ((w_main.shape[1] - inner) // tn, "conv_silu"))
            main, small = _in_proj(xin, p["norm_mix_pre"][l], sc1, sh1, w_main, w["ssd_in_dt"][j],
                                   kinds=kinds, conv_w=cw, conv_b=cb, tn=tn)
            y, s = _ssd_core(main, small, p["ssd_dt_bias"][j], p["ssd_a_log"][j], p["ssd_d"][j], p["ssd_norm_w"][j],
                             None if emit else states["ssd"][:, j], batch=batch, seq=seq, emit_state=emit)
            finals["ssd"].append(s)
            w_out = w["ssd_out"][j]
        elif kind == 1:
            w_main = w["ml_in_main"][j]
            d_out = w["ml_out"][j].shape[0]
            kinds = (((w_main.shape[1] - d_out) // tn, "none"), (d_out // tn, "sigmoid"))
            main, small = _in_proj(xin, p["norm_mix_pre"][l], sc1, sh1, w_main, w["ml_in_gates"][j], kinds=kinds, tn=tn)
            st = None if emit else (states["ml_c"][:, j], states["ml_n"][:, j], states["ml_m"][:, j])
            y, s = _mlstm_core(main, small, p["ml_igate_b"][j], p["ml_fgate_b"][j], p["ml_norm_w"][j], st,
                               batch=batch, seq=seq, emit_state=emit)
            if emit:
                finals["ml_c"].append(s[0])
                finals["ml_n"].append(s[1])
                finals["ml_m"].append(s[2])
            w_out = w["ml_out"][j]
        else:
            w_main = w["lru_in"][j]
            width = w["lru_out"][j].shape[0]
            cw, cb = _conv_over(width, p["lru_conv_w"][j], p["lru_conv_b"][j])
            kinds = ((width // tn, "gelu"), (width // tn, "conv_none"))
            main, _ = _in_proj(xin, p["norm_mix_pre"][l], sc1, sh1, w_main, kinds=kinds, conv_w=cw, conv_b=cb, tn=tn)
            y, s = _lru_core(main, w["lru_wa"][j], p["lru_ba"][j], w["lru_wi"][j], p["lru_bi"][j], p["lru_lambda"][j],
                             None if emit else states["lru"][:, j], batch=batch, seq=seq, emit_state=emit)
            finals["lru"].append(s)
            w_out = w["lru_out"][j]
        xo = _out_proj(y, w_out, xin, g1, p["norm_mix_post"][l])
        x = _to_row_major(xo) if col else xo
        x = _mlp(x, p["norm_mlp_pre"][l], sc2, sh2, w["mlp_w1"][l], w["mlp_w2"][l], g2, p["norm_mlp_post"][l])
    if emit:
        return x, {k: jnp.stack(v, axis=1) for k, v in finals.items()}
    return x, None


def kernel(x_prompt, x_sample, state_ssd, state_mlstm_C, state_mlstm_n, state_mlstm_m, state_rglru, c, c_ctx, mod_w, mod_b, norm_mix_pre, norm_mix_post, norm_mlp_pre, norm_mlp_post, mlp_w1, mlp_w2, ssd_in_w, ssd_conv_w, ssd_conv_b, ssd_dt_bias, ssd_a_log, ssd_d, ssd_norm_w, ssd_out_w, ml_in_w, ml_igate_b, ml_fgate_b, ml_norm_w, ml_out_w, lru_in_w, lru_conv_w, lru_conv_b, lru_wa, lru_ba, lru_wi, lru_bi, lru_lambda, lru_out_w):
    p = dict(norm_mix_pre=norm_mix_pre, norm_mix_post=norm_mix_post, norm_mlp_pre=norm_mlp_pre,
             norm_mlp_post=norm_mlp_post, ssd_conv_w=ssd_conv_w, ssd_conv_b=ssd_conv_b, ssd_dt_bias=ssd_dt_bias,
             ssd_a_log=ssd_a_log, ssd_d=ssd_d, ssd_norm_w=ssd_norm_w, ml_igate_b=ml_igate_b, ml_fgate_b=ml_fgate_b,
             ml_norm_w=ml_norm_w, lru_conv_w=lru_conv_w, lru_conv_b=lru_conv_b, lru_ba=lru_ba, lru_bi=lru_bi,
             lru_lambda=lru_lambda)
    d_model = x_prompt.shape[-1]
    depth = mod_w.shape[0]
    ssd_main = ssd_in_w.shape[2] - LANES
    ml_main = ml_in_w.shape[2] - 4 * state_mlstm_m.shape[-1]
    bf = lambda a: a.astype(BF16)
    n_ml, ml_heads = ml_in_w.shape[0], state_mlstm_m.shape[-1]
    ml_gates = ml_in_w[:, :, ml_main:].reshape(n_ml, d_model, 4, ml_heads).transpose(0, 3, 2, 1)
    ml_gates = jnp.pad(ml_gates, ((0, 0), (0, 0), (0, 4), (0, 0))).reshape(n_ml, 8 * ml_heads, d_model)
    ml_gates = jnp.pad(ml_gates, ((0, 0), (0, LANES - 8 * ml_heads), (0, 0)))
    w = dict(
        mlp_w1=bf(mlp_w1), mlp_w2=bf(mlp_w2),
        ssd_in_main=bf(ssd_in_w[:, :, :ssd_main]), ssd_in_dt=bf(ssd_in_w[:, :, ssd_main:].transpose(0, 2, 1)),
        ssd_out=bf(ssd_out_w), ml_in_main=bf(ml_in_w[:, :, :ml_main]), ml_in_gates=bf(ml_gates),
        ml_out=bf(ml_out_w), lru_in=bf(lru_in_w), lru_out=bf(lru_out_w), lru_wa=bf(lru_wa), lru_wi=bf(lru_wi),
    )
    n_dec = c.shape[0]
    n_rows = -(-(1 + n_dec) // SUBLANES) * SUBLANES
    cond = jnp.concatenate([c_ctx[None].astype(F32), c.astype(F32),
                            jnp.zeros((n_rows - 1 - n_dec, d_model), F32)], axis=0)
    mod = _mod_all(cond, mod_w, mod_b).reshape(depth, n_rows, 6, d_model)
    y_prompt, fin = _trunk(x_prompt, mod[:, 0:1], False, None, p, w)
    states = dict(ssd=state_ssd, ml_c=state_mlstm_C, ml_n=state_mlstm_n, ml_m=state_mlstm_m, lru=state_rglru)
    y_sample, _ = _trunk(x_sample, mod[:, 1:1 + n_dec], True, states, p, w)
    dt = x_prompt.dtype
    return (y_prompt, y_sample, fin["ssd"].astype(dt), fin["ml_c"].astype(dt), fin["ml_n"].astype(dt),
            fin["ml_m"].astype(dt), fin["lru"].astype(dt))
```

```python
import functools

import jax
import jax.numpy as jnp
from jax import lax
from jax.experimental import pallas as pl
from jax.experimental.pallas import tpu as pltpu

F32 = jnp.float32
BF16 = jnp.bfloat16
HIGHEST = lax.Precision.HIGHEST

EPS = 1e-6
CHUNK = 128
GRID_W = 64
N_MIXERS = 3
LRU_C = 8.0
SUBLANES = 8
LANES = 128
HALO = 16
CONV_ROWS = 256
VMEM_LIMIT_BYTES = 56 * 1024 * 1024


def _cparams(*semantics):
    return pltpu.CompilerParams(dimension_semantics=semantics, vmem_limit_bytes=VMEM_LIMIT_BYTES)


def _dot(a, b):
    return jnp.dot(a.astype(BF16), b.astype(BF16), preferred_element_type=F32)


def _dot_exact(a, b):
    return jnp.dot(a, b, precision=HIGHEST, preferred_element_type=F32)


def _softplus(x):
    return jnp.maximum(x, 0.0) + jnp.log1p(jnp.exp(-jnp.abs(x)))


def _sigmoid(x):
    return jax.nn.sigmoid(x)


def _silu(x):
    return x * jax.nn.sigmoid(x)


def _rms(x, w):
    return x * lax.rsqrt(jnp.mean(x * x, axis=-1, keepdims=True) + EPS) * w


def _tri(n, lower):
    r = lax.broadcasted_iota(jnp.int32, (n, n), 0)
    c = lax.broadcasted_iota(jnp.int32, (n, n), 1)
    return (r >= c if lower else r <= c).astype(F32)


def _split3(v):
    hi = v.astype(BF16).astype(F32)
    r = v - hi
    mid = r.astype(BF16).astype(F32)
    return hi, mid, r - mid


_ACTS = {
    "none": lambda v: v,
    "silu": _silu,
    "sigmoid": _sigmoid,
    "gelu": lambda v: jax.nn.gelu(v, approximate=True),
}


def _mod_kernel(c_ref, w_ref, b_ref, o_ref):
    a = _silu(c_ref[...])
    o_ref[0] = _dot(a, w_ref[0]) + b_ref[0]


def _mod_all(cond, mod_w, mod_b):
    depth, d, n = mod_w.shape
    r = cond.shape[0]
    tn = 1024
    return pl.pallas_call(
        _mod_kernel,
        grid=(depth, n // tn),
        in_specs=[
            pl.BlockSpec((r, d), lambda l, j: (0, 0)),
            pl.BlockSpec((1, d, tn), lambda l, j: (l, 0, j)),
            pl.BlockSpec((1, 1, tn), lambda l, j: (l, 0, j)),
        ],
        out_specs=pl.BlockSpec((1, r, tn), lambda l, j: (l, 0, j)),
        out_shape=jax.ShapeDtypeStruct((depth, r, n), F32),
        compiler_params=_cparams("parallel", "parallel"),
        name="mod_map",
    )(cond, mod_w, mod_b.reshape(depth, 1, n))


def _in_proj_kernel(*refs, kinds, seq, tm, halo, has_conv, has_small):
    refs = list(refs)
    x_ref = refs.pop(0)
    xp_ref, xn_ref = (refs.pop(0), refs.pop(0)) if halo else (None, None)
    nw_ref, sc_ref, sh_ref, w_ref = refs.pop(0), refs.pop(0), refs.pop(0), refs.pop(0)
    cw_ref, cb_ref = (refs.pop(0), refs.pop(0)) if has_conv else (None, None)
    ws_ref = refs.pop(0) if has_small else None
    o_ref = refs.pop(0)
    os_ref = refs.pop(0) if has_small else None
    (h_ref,) = refs
    i, j = pl.program_id(0), pl.program_id(1)
    pad = HALO if halo else 0
    main = pl.ds(pad, tm)

    @pl.when(j == 0)
    def _():
        norm = lambda v: (_rms(v, nw_ref[...]) * (1.0 + sc_ref[0]) + sh_ref[0]).astype(BF16)
        h_ref[main, :] = norm(x_ref[...])
        if halo:
            h_ref[pl.ds(0, pad), :] = norm(xp_ref[...])
            h_ref[pl.ds(pad + tm, pad), :] = norm(xn_ref[...])
        if has_small:
            os_ref[...] = lax.dot_general(ws_ref[...], h_ref[main, :], (((1,), (1,)), ((), ())),
                                          preferred_element_type=F32)

    def plain(act):
        o_ref[...] = act(jnp.dot(h_ref[main, :], w_ref[0].astype(BF16), preferred_element_type=F32))

    def conv(act):
        w = cw_ref[...]
        wm = w_ref[0].astype(BF16)
        rb = min(CONV_ROWS, tm) if halo else min(tm, max(CONV_ROWS, seq))
        for blk in range(tm // rb):
            lo = blk * rb
            r = jnp.dot(h_ref[pl.ds(lo, rb + 2 * pad), :], wm, preferred_element_type=F32)
            n = r.shape[0]
            if halo:
                lead, tail = r[:pad], r[pad + rb:]
                if blk == 0:
                    lead = jnp.where((i * tm) % seq == 0, 0.0, lead)
                if blk == tm // rb - 1:
                    tail = jnp.where(((i + 1) * tm) % seq == 0, 0.0, tail)
                r = jnp.concatenate([lead, r[pad:pad + rb], tail], axis=0)
            tap = lambda shift: pltpu.roll(r, shift % n, 0)[pad:pad + rb]
            t2, t1, tn1 = tap(2), tap(1), tap(-1)
            if not halo:
                pos = (lo + lax.broadcasted_iota(jnp.int32, r.shape, 0)) & (seq - 1)
                t2 = jnp.where(pos >= 2, t2, 0.0)
                t1 = jnp.where(pos >= 1, t1, 0.0)
                tn1 = jnp.where(pos < seq - 1, tn1, 0.0)
            o_ref[pl.ds(lo, rb), :] = act(w[0:1] * t2 + w[1:2] * t1 + w[2:3] * r[pad:pad + rb] + w[3:4] * tn1
                                          + cb_ref[...])

    start = 0
    for count, kind in kinds:
        body = (functools.partial(conv, _ACTS[kind[5:]]) if kind.startswith("conv_")
                else functools.partial(plain, _ACTS[kind]))
        pl.when((j >= start) & (j < start + count))(body)
        start += count


def _in_proj(x, norm_w, scale, shift, w_all, w_small=None, *, layer, n, kinds, conv_w=None, conv_b=None,
             tm=1024, tn=512):
    b, l, d = x.shape
    m = b * l
    per_batch = scale.shape[0] > 1
    assert m % tm == 0 and n % tn == 0 and (not per_batch or l % tm == 0)
    assert sum(c for c, _ in kinds) == n // tn
    has_conv = any(k.startswith("conv_") for _, k in kinds)
    halo = has_conv and l > tm
    assert not has_conv or (l % tm == 0 if halo else (tm % l == 0 and l & (l - 1) == 0))
    mod_idx = (lambda i, j: ((i * tm) // l, 0, 0)) if per_batch else (lambda i, j: (0, 0, 0))
    has_small = w_small is not None
    x2 = x.reshape(m, d)
    in_specs = [pl.BlockSpec((tm, d), lambda i, j: (i, 0))]
    args = [x2]
    if halo:
        per_tile, last = tm // HALO, m // HALO - 1
        in_specs += [pl.BlockSpec((HALO, d), lambda i, j: (jnp.maximum(i * per_tile - 1, 0), 0)),
                     pl.BlockSpec((HALO, d), lambda i, j: (jnp.minimum((i + 1) * per_tile, last), 0))]
        args += [x2, x2]
    in_specs += [
        pl.BlockSpec((1, d), lambda i, j: (0, 0)),
        pl.BlockSpec((1, 1, d), mod_idx),
        pl.BlockSpec((1, 1, d), mod_idx),
        pl.BlockSpec((1, d, tn), lambda i, j: (layer, 0, j)),
    ]
    args += [norm_w.reshape(1, d), scale, shift, w_all]
    if has_conv:
        in_specs += [pl.BlockSpec((4, tn), lambda i, j: (0, j)), pl.BlockSpec((1, tn), lambda i, j: (0, j))]
        args += [conv_w.astype(F32), conv_b.astype(F32).reshape(1, n)]
    out_specs = [pl.BlockSpec((tm, tn), lambda i, j: (i, j))]
    out_shape = [jax.ShapeDtypeStruct((m, n), F32)]
    if has_small:
        ns = w_small.shape[0]
        in_specs.append(pl.BlockSpec((ns, d), lambda i, j: (0, 0)))
        out_specs.append(pl.BlockSpec((ns, tm), lambda i, j: (0, i)))
        out_shape.append(jax.ShapeDtypeStruct((ns, m), F32))
        args.append(w_small)
    outs = pl.pallas_call(
        functools.partial(_in_proj_kernel, kinds=tuple(kinds), seq=l, tm=tm, halo=halo, has_conv=has_conv,
                          has_small=has_small),
        grid=(m // tm, n // tn),
        in_specs=in_specs,
        out_specs=out_specs,
        out_shape=out_shape,
        scratch_shapes=[pltpu.VMEM((tm + (2 * HALO if halo else 0), d), BF16)],
        compiler_params=_cparams("parallel", "arbitrary"),
        name="in_proj",
    )(*args)
    return (outs[0], outs[1]) if has_small else (outs[0], None)


def _out_proj_kernel(y_ref, w_ref, x_ref, g_ref, nw_ref, o_ref, *acc, nk):
    k = pl.program_id(1)
    part = jnp.dot(y_ref[...], w_ref[...], preferred_element_type=F32)
    finish = lambda f: x_ref[...] + g_ref[0] * _rms(f, nw_ref[...])
    if nk == 1:
        o_ref[...] = finish(part)
        return
    (acc_ref,) = acc

    @pl.when(k == 0)
    def _():
        acc_ref[...] = part

    @pl.when((k > 0) & (k < nk - 1))
    def _():
        acc_ref[...] += part

    @pl.when(k == nk - 1)
    def _():
        o_ref[...] = finish(acc_ref[...] + part)


def _out_proj(y, w, x, gate, norm_w, *, tm=512, tk=2048):
    b, l, d = x.shape
    m = b * l
    kin = w.shape[0]
    per_batch = gate.shape[0] > 1
    assert m % tm == 0 and kin % tk == 0 and (not per_batch or l % tm == 0)
    mod_idx = (lambda i, k: ((i * tm) // l, 0, 0)) if per_batch else (lambda i, k: (0, 0, 0))
    nk = kin // tk
    out = pl.pallas_call(
        functools.partial(_out_proj_kernel, nk=nk),
        grid=(m // tm, nk),
        in_specs=[
            pl.BlockSpec((tm, tk), lambda i, k: (i, k)),
            pl.BlockSpec((tk, d), lambda i, k: (k, 0)),
            pl.BlockSpec((tm, d), lambda i, k: (i, 0)),
            pl.BlockSpec((1, 1, d), mod_idx),
            pl.BlockSpec((1, d), lambda i, k: (0, 0)),
        ],
        out_specs=pl.BlockSpec((tm, d), lambda i, k: (i, 0)),
        out_shape=jax.ShapeDtypeStruct((m, d), F32),
        scratch_shapes=[pltpu.VMEM((tm, d), F32)] if nk > 1 else [],
        compiler_params=_cparams("parallel", "arbitrary"),
        name="out_proj",
    )(y.reshape(m, kin), w, x.reshape(m, d), gate, norm_w.reshape(1, d))
    return out.reshape(b, l, d)


def _mlp_kernel(x_ref, nw_ref, sc_ref, sh_ref, w1_ref, w2_ref, g_ref, pw_ref, o_ref, h_ref, acc_ref, *, nf):
    j = pl.program_id(1)

    @pl.when(j == 0)
    def _():
        h = _rms(x_ref[...], nw_ref[...]) * (1.0 + sc_ref[0]) + sh_ref[0]
        h_ref[...] = h.astype(BF16)
        acc_ref[...] = jnp.zeros_like(acc_ref)

    t = jnp.dot(h_ref[...], w1_ref[...], preferred_element_type=F32)
    t = jnp.square(jnp.maximum(t, 0.0))
    acc_ref[...] += jnp.dot(t.astype(BF16), w2_ref[...], preferred_element_type=F32)

    @pl.when(j == nf - 1)
    def _():
        o_ref[...] = x_ref[...] + g_ref[0] * _rms(acc_ref[...], pw_ref[...])


def _mlp(x, pre_w, scale, shift, w1, w2, gate, post_w, *, tm=512, tf=512):
    b, l, d = x.shape
    m = b * l
    dff = w1.shape[1]
    per_batch = scale.shape[0] > 1
    assert m % tm == 0 and dff % tf == 0 and (not per_batch or l % tm == 0)
    mod_idx = (lambda i, j: ((i * tm) // l, 0, 0)) if per_batch else (lambda i, j: (0, 0, 0))
    nf = dff // tf
    out = pl.pallas_call(
        functools.partial(_mlp_kernel, nf=nf),
        grid=(m // tm, nf),
        in_specs=[
            pl.BlockSpec((tm, d), lambda i, j: (i, 0)),
            pl.BlockSpec((1, d), lambda i, j: (0, 0)),
            pl.BlockSpec((1, 1, d), mod_idx),
            pl.BlockSpec((1, 1, d), mod_idx),
            pl.BlockSpec((d, tf), lambda i, j: (0, j)),
            pl.BlockSpec((tf, d), lambda i, j: (j, 0)),
            pl.BlockSpec((1, 1, d), mod_idx),
            pl.BlockSpec((1, d), lambda i, j: (0, 0)),
        ],
        out_specs=pl.BlockSpec((tm, d), lambda i, j: (i, 0)),
        out_shape=jax.ShapeDtypeStruct((m, d), F32),
        scratch_shapes=[pltpu.VMEM((tm, d), BF16), pltpu.VMEM((tm, d), F32)],
        compiler_params=_cparams("parallel", "arbitrary"),
        name="mlp",
    )(x.reshape(m, d), pre_w.reshape(1, d), scale, shift, w1, w2, gate, post_w.reshape(1, d))
    return out.reshape(b, l, d)


SSD_GROUP_HEADS = 8
SSD_HEAD_DIM = 64


def _ssd_kernel(*refs, seq, has_h0, emit_state):
    z_ref, x_ref, b_ref, c_ref, dtf_ref, dtb_ref, biast_ref, alogt_ref, dskip_ref, nw_ref = refs[:10]
    rest = list(refs[10:])
    h0_ref = rest.pop(0) if has_h0 else None
    y_ref = rest.pop(0)
    st_ref = rest.pop(0) if emit_state else None
    ya_ref, yt_ref, hf_ref, hb_ref, ct_ref, xt_ref, qry_ref, keyf_ref, keyb_ref, dts_ref, erow_ref, wrow_ref = rest
    n_chunks = seq // CHUNK
    gh, hd = SSD_GROUP_HEADS, SSD_HEAD_DIM
    hp = gh * hd
    wide = gh * CHUNK
    li = lax.broadcasted_iota(jnp.int32, (CHUNK, CHUNK), 0)
    si = lax.broadcasted_iota(jnp.int32, (CHUNK, CHUNK), 1)
    lane = lax.broadcasted_iota(jnp.int32, (CHUNK, LANES), 1)

    onehot = lambda cond: jnp.where(cond, 1.0, 0.0).astype(BF16)
    k_i = lax.broadcasted_iota(jnp.int32, (48, wide), 0)
    head_of_lane = lax.broadcasted_iota(jnp.int32, (48, wide), 1) >> 7
    pick_f = onehot((k_i & 15) == head_of_lane)
    pick_b = onehot((k_i & 15) == gh + head_of_lane)
    zero_tail = jnp.zeros((48, wide), BF16)
    diag_blocks = (lax.broadcasted_iota(jnp.int32, (gh, wide), 0)
                   == lax.broadcasted_iota(jnp.int32, (gh, wide), 1) >> 7)

    def per_head(t, rows):
        return jnp.concatenate([t[hd * h:hd * (h + 1)] * rows[h:h + 1] for h in range(gh)], axis=0)

    def key_rows(parts, d):
        blocks = [jnp.where(diag_blocks, -jnp.tile(p[d:d + gh], (1, gh)), 0.0) for p in parts]
        blocks.append(jnp.zeros((gh, wide), F32))
        return jnp.concatenate(blocks, axis=0).astype(BF16)

    dt_all = _softplus(jnp.concatenate([dtf_ref[...], dtb_ref[...]], axis=0) + biast_ref[0])
    dts_ref[...] = dt_all
    stack = lambda v: jnp.concatenate([v[:, CHUNK * c:CHUNK * (c + 1)] for c in range(n_chunks)], axis=0)
    dt_st = stack(dt_all)
    a_st = stack(dt_all * (-jnp.exp(alogt_ref[0])))
    cs = _dot_exact(a_st, _tri(CHUNK, False))
    tot = cs[:, CHUNK - 1:CHUNK]
    backward = (lax.broadcasted_iota(jnp.int32, cs.shape, 0) & (2 * gh - 1)) >= gh
    pos = jnp.where(backward, tot - cs + a_st, cs)
    erow_ref[...] = jnp.exp(pos)
    wrow_ref[...] = dt_st * jnp.exp(tot - pos)
    pos3 = _split3(pos)
    for c in range(n_chunks):
        blk, tok = slice(2 * gh * c, 2 * gh * (c + 1)), slice(CHUNK * c, CHUNK * (c + 1))
        parts = [p[blk] for p in pos3]
        qry_ref[tok, :] = jnp.concatenate([*parts, jnp.ones((32, CHUNK), F32), jnp.zeros((48, CHUNK), F32)],
                                          axis=0).T.astype(BF16)
        keyf_ref[32 * c:32 * (c + 1), :] = key_rows(parts, 0)
        keyb_ref[32 * c:32 * (c + 1), :] = key_rows(parts, gh)

    def within(c, carry):
        t0 = pl.multiple_of(c * CHUNK, CHUNK)
        rows = pl.ds(t0, CHUNK)
        x = x_ref[0, rows, :]
        cm = c_ref[0, rows, :]
        bt = b_ref[0, rows, :].T
        ct_ref[rows, :] = cm.T.astype(BF16)
        xt_ref[pl.ds(pl.multiple_of(c * hp, hp), hp), :] = x.T
        dtt = dts_ref[:, rows]
        query = qry_ref[rows, :]
        keys = pl.ds(pl.multiple_of(c * 32, 32), 32)
        arg_f = jnp.dot(query, jnp.concatenate([pick_f, keyf_ref[keys, :], zero_tail], axis=0),
                        preferred_element_type=F32)
        arg_b = jnp.dot(query, jnp.concatenate([pick_b, keyb_ref[keys, :], zero_tail], axis=0),
                        preferred_element_type=F32)
        cb = _dot(cm, bt)
        ydiag = []
        for p in range(gh // 2):
            wms = []
            for h in (2 * p, 2 * p + 1):
                arg = jnp.where(si <= li, arg_f[:, CHUNK * h:CHUNK * (h + 1)], arg_b[:, CHUNK * h:CHUNK * (h + 1)])
                coef = jnp.where(si < li, dtt[h:h + 1, :],
                                 jnp.where(si > li, dtt[gh + h:gh + h + 1, :], dtt[h:h + 1, :] + dtt[gh + h:gh + h + 1, :]))
                wms.append(cb * jnp.exp(arg) * coef)
            xp = x[:, LANES * p:LANES * (p + 1)]
            rhs = jnp.concatenate([jnp.where(lane < hd, xp, 0.0), jnp.where(lane >= hd, xp, 0.0)], axis=0)
            ydiag.append(_dot(jnp.concatenate(wms, axis=1), rhs))
        ya_ref[rows, :] = jnp.concatenate(ydiag, axis=1)
        return carry

    lax.fori_loop(0, n_chunks, within, 0, unroll=min(4, n_chunks))

    for d, st in ((0, hf_ref), (1, hb_ref)):
        st[...] = h0_ref[0, d].reshape(hp, -1) if has_h0 else jnp.zeros_like(st)
    yt_ref[...] = jnp.zeros_like(yt_ref)

    def carried(c, d, st, edge):
        rows = pl.ds(pl.multiple_of(c * CHUNK, CHUNK), CHUNK)
        chans = pl.ds(pl.multiple_of(c * hp, hp), hp)
        heads = pl.ds(pl.multiple_of(c * 2 * gh + d * gh, gh), gh)
        decay, weight = erow_ref[heads, :], wrow_ref[heads, :]
        h = st[...]
        yt_ref[chans, :] += per_head(_dot(h, ct_ref[rows, :]), decay)
        inc = _dot(per_head(xt_ref[chans, :], weight), b_ref[0, rows, :])
        st[...] = per_head(h, decay[:, edge:edge + 1]) + inc

    def across(i, carry):
        carried(i, 0, hf_ref, CHUNK - 1)
        carried(n_chunks - 1 - i, 1, hb_ref, 0)
        return carry

    lax.fori_loop(0, n_chunks, across, 0)
    if emit_state:
        for d, st in ((0, hf_ref), (1, hb_ref)):
            st_ref[0, d] = st[...].reshape(st_ref.shape[2:])

    def finish(c, carry):
        rows = pl.ds(pl.multiple_of(c * CHUNK, CHUNK), CHUNK)
        y = ya_ref[rows, :] + yt_ref[pl.ds(pl.multiple_of(c * hp, hp), hp), :].T
        y = (y + dskip_ref[...] * x_ref[0, rows, :]) * z_ref[0, rows, :]
        y_ref[0, rows, :] = _rms(y, nw_ref[...]).astype(BF16)
        return carry

    lax.fori_loop(0, n_chunks, finish, 0, unroll=2)


def _ssd_core(main, small, dt_bias, a_log, d_skip, norm_w, h0, *, batch, seq, emit_state):
    heads, groups, hd, hg = 64, 8, SSD_HEAD_DIM, SSD_GROUP_HEADS
    nstate = 128
    inner = heads * hd
    hp = inner // groups
    main = main.reshape(batch, seq, -1)

    def per_group(p):
        return p.astype(F32).reshape(2, groups, hg).transpose(1, 0, 2).reshape(groups, 2 * hg, 1)

    dsk = jnp.repeat(d_skip.astype(F32), hd).reshape(1, inner)
    nw = norm_w.astype(F32).reshape(1, inner)
    xo, bo, co = inner // hp, 2 * inner // nstate, (2 * inner + groups * nstate) // nstate
    in_specs = [
        pl.BlockSpec((1, seq, hp), lambda b, g: (b, 0, g)),
        pl.BlockSpec((1, seq, hp), lambda b, g: (b, 0, xo + g)),
        pl.BlockSpec((1, seq, nstate), lambda b, g: (b, 0, bo + g)),
        pl.BlockSpec((1, seq, nstate), lambda b, g: (b, 0, co + g)),
        pl.BlockSpec((hg, seq), lambda b, g: (g, b)),
        pl.BlockSpec((hg, seq), lambda b, g: (groups + g, b)),
        pl.BlockSpec((1, 2 * hg, 1), lambda b, g: (g, 0, 0)),
        pl.BlockSpec((1, 2 * hg, 1), lambda b, g: (g, 0, 0)),
        pl.BlockSpec((1, hp), lambda b, g: (0, g)),
        pl.BlockSpec((1, hp), lambda b, g: (0, g)),
    ]
    args = [main, main, main, main, small, small, per_group(dt_bias), per_group(a_log), dsk, nw]
    state_spec = pl.BlockSpec((1, 2, hg, hd, nstate), lambda b, g: (b, 0, g, 0, 0))
    if h0 is not None:
        in_specs.append(state_spec)
        args.append(h0.astype(F32))
    out_specs = [pl.BlockSpec((1, seq, hp), lambda b, g: (b, 0, g))]
    out_shape = [jax.ShapeDtypeStruct((batch, seq, inner), BF16)]
    if emit_state:
        out_specs.append(state_spec)
        out_shape.append(jax.ShapeDtypeStruct((batch, 2, heads, hd, nstate), F32))
    outs = pl.pallas_call(
        functools.partial(_ssd_kernel, seq=seq, has_h0=h0 is not None, emit_state=emit_state),
        grid=(batch, groups),
        in_specs=in_specs,
        out_specs=out_specs,
        out_shape=out_shape,
        scratch_shapes=[pltpu.VMEM((seq, hp), F32), pltpu.VMEM((seq // CHUNK * hp, CHUNK), F32),
                        pltpu.VMEM((hp, nstate), F32), pltpu.VMEM((hp, nstate), F32),
                        pltpu.VMEM((seq // CHUNK * nstate, CHUNK), BF16), pltpu.VMEM((seq // CHUNK * hp, CHUNK), F32),
                        pltpu.VMEM((seq, LANES), BF16),
                        pltpu.VMEM((seq // CHUNK * 32, hg * CHUNK), BF16), pltpu.VMEM((seq // CHUNK * 32, hg * CHUNK), BF16),
                        pltpu.VMEM((2 * hg, seq), F32),
                        pltpu.VMEM((seq // CHUNK * 2 * hg, CHUNK), F32), pltpu.VMEM((seq // CHUNK * 2 * hg, CHUNK), F32)],
        compiler_params=_cparams("parallel", "parallel"),
        name="ssd_core",
    )(*args)
    return (outs[0], outs[1]) if emit_state else (outs[0], None)


def _mlstm_kernel(*refs, seq, has_state, emit_state):
    q_ref, k_ref, v_ref, o_ref, gr_ref, br_ref, nw_ref = refs[:7]
    rest = list(refs[7:])
    if has_state:
        c0_ref, n0_ref, m0_ref = rest.pop(0), rest.pop(0), rest.pop(0)
    y_ref = rest.pop(0)
    if emit_state:
        cf_ref, nf_ref, mf_ref = rest.pop(0), rest.pop(0), rest.pop(0)
    sv_ref, rsum_ref, cmxb_ref, accb_ref, u_ref, rows_ref, cmx_ref, cols_ref, caf_ref, cab_ref = rest
    n_chunks = seq // CHUNK
    dqk = q_ref.shape[2]
    dv = v_ref.shape[2]
    li = lax.broadcasted_iota(jnp.int32, (CHUNK, CHUNK), 0)
    si = lax.broadcasted_iota(jnp.int32, (CHUNK, CHUNK), 1)
    ones_blk = jnp.ones((CHUNK, LANES), F32)
    k_scale = dqk ** -0.5

    def load_state(ca_ref, d):
        if has_state:
            ca_ref[...] = jnp.concatenate([c0_ref[0, d, 0], jnp.broadcast_to(n0_ref[0, d, 0], (dqk, LANES))], axis=1)
            return m0_ref[0, d, 0]
        ca_ref[...] = jnp.zeros_like(ca_ref)
        return jnp.zeros((1, 1), F32)

    def store_state(ca_ref, d, m):
        if emit_state:
            cf_ref[0, d, 0] = ca_ref[:, :dv]
            nf_ref[0, d, 0] = ca_ref[:, dv:dv + 1]
            mf_ref[0, d, 0] = m


    g_all = gr_ref[...] + br_ref[0]
    g_st = jnp.concatenate([g_all[:, CHUNK * c:CHUNK * (c + 1)] for c in range(n_chunks)], axis=0)
    r8 = lax.broadcasted_iota(jnp.int32, g_st.shape, 0) & 7
    tok = lax.broadcasted_iota(jnp.int32, g_st.shape, 1)
    g_st = jnp.where((r8 == 1) | (r8 == 3), -_softplus(-g_st), g_st)
    cum = _dot_exact(g_st, _tri(CHUNK, False))
    tot = cum[:, CHUNK - 1:CHUNK]
    acc = jnp.where(r8 == 1, cum, jnp.where(r8 == 3, tot - cum + g_st, 0.0))
    neg = g_st - pltpu.roll(acc, acc.shape[0] - 1, 0)
    pmax, smax = neg, neg
    for k in (1, 2, 4, 8, 16, 32, 64):
        pmax = jnp.where(tok >= k, jnp.maximum(pmax, pltpu.roll(pmax, k, 1)), pmax)
        smax = jnp.where(tok < CHUNK - k, jnp.maximum(smax, pltpu.roll(smax, CHUNK - k, 1)), smax)
    cmx = jnp.where(r8 == 0, pmax, jnp.where(r8 == 2, smax, 0.0))
    terms_all = jnp.where((r8 == 0) | (r8 == 2), neg, acc)
    rows_ref[...] = terms_all
    cmx_ref[...] = cmx
    for c in range(n_chunks):
        blk = slice(8 * c, 8 * (c + 1))
        cols_ref[CHUNK * c:CHUNK * (c + 1), :] = jnp.concatenate(
            [terms_all[blk], cmx[blk], jnp.zeros((CHUNK - 16, CHUNK), F32)], axis=0).T

    def within(c, carry):
        rows = pl.ds(pl.multiple_of(c * CHUNK, CHUNK), CHUNK)
        r8c = pl.ds(pl.multiple_of(c * 8, 8), 8)
        q = q_ref[0, rows, :]
        kt = (k_ref[0, rows, :] * k_scale).T
        v = v_ref[0, rows, :]
        vaug = jnp.concatenate([v, ones_blk], axis=1)
        qk = _dot(q, kt)
        terms, cmxr, cols = rows_ref[r8c, :], cmx_ref[r8c, :], cols_ref[rows, :]
        for d in (0, 1):
            neg_row = terms[2 * d:2 * d + 1]
            cmx_rep = jnp.broadcast_to(cols[:, 8 + 2 * d:9 + 2 * d], (CHUNK, LANES))
            cmxb_ref[d, rows, :] = cmx_rep
            accb_ref[d, rows, :] = jnp.broadcast_to(cols[:, 1 + 2 * d:2 + 2 * d], (CHUNK, LANES))
            s0 = jnp.where(si >= li if d else si <= li, qk * jnp.exp(neg_row - cmx_rep), 0.0)
            sv_ref[d, rows, :] = _dot(s0, v)
            rsum_ref[d, rows, :] = jnp.broadcast_to(jnp.sum(s0, axis=1, keepdims=True), (CHUNK, LANES))
            maxneg = cmxr[2:3, 0:1] if d else cmxr[0:1, CHUNK - 1:CHUNK]
            u_ref[d, pl.ds(pl.multiple_of(c * dqk, dqk), dqk), :] = _dot(kt * jnp.exp(neg_row - maxneg), vaug)
        return carry

    lax.fori_loop(0, n_chunks, within, 0, unroll=2)

    def carried(c, d, ca_ref, m):
        rows = pl.ds(pl.multiple_of(c * CHUNK, CHUNK), CHUNK)
        r8c = pl.ds(pl.multiple_of(c * 8, 8), 8)
        ca = ca_ref[...]
        qc = _dot(q_ref[0, rows, :], ca)
        cmx_rep = cmxb_ref[d, rows, :]
        big = jnp.maximum(m, cmx_rep)
        r = jnp.exp(cmx_rep - big)
        w_in = jnp.exp(m - big)
        den = r * rsum_ref[d, rows, :] + w_in * qc[:, dv:]
        inv = 1.0 / jnp.maximum(jnp.abs(den), jnp.exp(-(accb_ref[d, rows, :] + big)))
        wide = lambda t: jnp.concatenate([t] * (dv // LANES), axis=1)
        sv_ref[d, rows, :] = wide(r * inv) * sv_ref[d, rows, :] + wide(w_in * inv) * qc[:, :dv]
        terms, cmxr = rows_ref[r8c, :], cmx_ref[r8c, :]
        tot_c = terms[3:4, 0:1] if d else terms[1:2, CHUNK - 1:CHUNK]
        maxneg = cmxr[2:3, 0:1] if d else cmxr[0:1, CHUNK - 1:CHUNK]
        top = jnp.maximum(m, maxneg)
        ca_ref[...] = jnp.exp(m - top) * ca + jnp.exp(maxneg - top) * u_ref[d, pl.ds(pl.multiple_of(c * dqk, dqk), dqk), :]
        return tot_c + top

    def across(i, ms):
        return carried(i, 0, caf_ref, ms[0]), carried(n_chunks - 1 - i, 1, cab_ref, ms[1])

    m_f, m_b = lax.fori_loop(0, n_chunks, across, (load_state(caf_ref, 0), load_state(cab_ref, 1)))
    store_state(caf_ref, 0, m_f)
    store_state(cab_ref, 1, m_b)

    def finish(c, carry):
        rows = pl.ds(pl.multiple_of(c * CHUNK, CHUNK), CHUNK)
        h = _rms(sv_ref[0, rows, :] + sv_ref[1, rows, :], nw_ref[...])
        y_ref[0, rows, :] = (h * o_ref[0, rows, :]).astype(BF16)
        return carry

    lax.fori_loop(0, n_chunks, finish, 0)


def _mlstm_core(main, small, igate_b, fgate_b, norm_w, state, *, batch, seq, emit_state):
    heads, dqk, dv = 8, 128, 256
    main = main.reshape(batch, seq, -1)
    bias = jnp.stack([igate_b[0], fgate_b[0], igate_b[1], fgate_b[1]], axis=1).astype(F32)
    bias = jnp.pad(bias, ((0, 0), (0, 4)))
    in_specs = [
        pl.BlockSpec((1, seq, dqk), lambda b, h: (b, 0, h)),
        pl.BlockSpec((1, seq, dqk), lambda b, h: (b, 0, heads + h)),
        pl.BlockSpec((1, seq, dv), lambda b, h: (b, 0, heads + h)),
        pl.BlockSpec((1, seq, dv), lambda b, h: (b, 0, 2 * heads + h)),
        pl.BlockSpec((8, seq), lambda b, h: (h, b)),
        pl.BlockSpec((1, 8, 1), lambda b, h: (h, 0, 0)),
        pl.BlockSpec((1, dv), lambda b, h: (0, h)),
    ]
    args = [main, main, main, main, small, bias.reshape(heads, 8, 1), norm_w.astype(F32).reshape(1, heads * dv)]
    c_spec = pl.BlockSpec((1, 2, 1, dqk, dv), lambda b, h: (b, 0, h, 0, 0))
    n_spec = pl.BlockSpec((1, 2, 1, dqk, 1), lambda b, h: (b, 0, h, 0, 0))
    m_spec = pl.BlockSpec((1, 2, 1, 1, 1), lambda b, h: (b, 0, h, 0, 0))
    if state is not None:
        c0, n0, m0 = state
        in_specs += [c_spec, n_spec, m_spec]
        args += [c0.astype(F32), n0.astype(F32).reshape(batch, 2, heads, dqk, 1),
                 m0.astype(F32).reshape(batch, 2, heads, 1, 1)]
    out_specs = [pl.BlockSpec((1, seq, dv), lambda b, h: (b, 0, h))]
    out_shape = [jax.ShapeDtypeStruct((batch, seq, heads * dv), BF16)]
    if emit_state:
        out_specs += [c_spec, n_spec, m_spec]
        out_shape += [jax.ShapeDtypeStruct((batch, 2, heads, dqk, dv), F32),
                      jax.ShapeDtypeStruct((batch, 2, heads, dqk, 1), F32),
                      jax.ShapeDtypeStruct((batch, 2, heads, 1, 1), F32)]
    outs = pl.pallas_call(
        functools.partial(_mlstm_kernel, seq=seq, has_state=state is not None, emit_state=emit_state),
        grid=(batch, heads),
        in_specs=in_specs,
        out_specs=out_specs,
        out_shape=out_shape,
        scratch_shapes=[pltpu.VMEM((2, seq, dv), F32), pltpu.VMEM((2, seq, LANES), F32),
                        pltpu.VMEM((2, seq, LANES), F32), pltpu.VMEM((2, seq, LANES), F32),
                        pltpu.VMEM((2, seq // CHUNK * dqk, dv + LANES), F32),
                        pltpu.VMEM((seq // CHUNK * 8, CHUNK), F32), pltpu.VMEM((seq // CHUNK * 8, CHUNK), F32),
                        pltpu.VMEM((seq, LANES), F32),
                        pltpu.VMEM((dqk, dv + LANES), F32), pltpu.VMEM((dqk, dv + LANES), F32)],
        compiler_params=_cparams("parallel", "parallel"),
        name="mlstm_core",
    )(*args)
    if emit_state:
        return outs[0], (outs[1], outs[2].reshape(batch, 2, heads, dqk), outs[3].reshape(batch, 2, heads))
    return outs[0], None


LRU_UNROLL = 8


def _scan8(a, b, reverse):
    row = lax.broadcasted_iota(jnp.int32, a.shape, 0)
    for k in (1, 2, 4):
        shift = SUBLANES - k if reverse else k
        valid = row < SUBLANES - k if reverse else row >= k
        a_prev = pltpu.roll(a, shift, 0)
        b_prev = pltpu.roll(b, shift, 0)
        b = jnp.where(valid, a * b_prev, 0.0) + b
        a = jnp.where(valid, a * a_prev, a)
    return a, b


def _lru_kernel(*refs, seq, rows, has_h0, emit_state):
    gate_ref, x_ref, wa_ref, wi_ref, ba_ref, bi_ref, lam_ref = refs[:7]
    rest = list(refs[7:])
    h0_ref = rest.pop(0) if has_h0 else None
    y_ref = rest.pop(0)
    st_ref = rest.pop(0) if emit_state else None
    hf_ref, hb_ref, af_ref, bf_ref, ab_ref, bb_ref = rest
    n_chunks = seq // rows
    n_groups = rows // SUBLANES
    width = x_ref.shape[2]

    def coefficients(c, d, a_ref, b_ref):
        xc = x_ref[0, pl.ds(pl.multiple_of(c * rows, rows), rows), :]
        r = _sigmoid(_dot(xc, wa_ref[d, 0]) + ba_ref[d])
        i = _sigmoid(_dot(xc, wi_ref[d, 0]) + bi_ref[d])
        log_a = -LRU_C * r * _softplus(-lam_ref[d])
        a = jnp.exp(log_a)
        a_ref[...] = a
        b_ref[...] = jnp.sqrt(-jnp.tanh(log_a) * (1.0 + a * a)) * (i * xc)

    def initial(d):
        return h0_ref[0, d] if has_h0 else jnp.zeros((1, width), F32)

    def step(i, carries):
        cf, cb = i, n_chunks - 1 - i
        coefficients(cf, 0, af_ref, bf_ref)
        coefficients(cb, 1, ab_ref, bb_ref)

        def group(gi, carries):
            carry_f, carry_b = carries
            r0 = pl.multiple_of(gi * SUBLANES, SUBLANES)
            a, b = _scan8(af_ref[pl.ds(r0, SUBLANES), :], bf_ref[pl.ds(r0, SUBLANES), :], False)
            h = b + a * carry_f
            hf_ref[pl.ds(pl.multiple_of(cf * rows + r0, SUBLANES), SUBLANES), :] = h
            carry_f = h[SUBLANES - 1:SUBLANES]
            r0 = pl.multiple_of((n_groups - 1 - gi) * SUBLANES, SUBLANES)
            a, b = _scan8(ab_ref[pl.ds(r0, SUBLANES), :], bb_ref[pl.ds(r0, SUBLANES), :], True)
            h = b + a * carry_b
            hb_ref[pl.ds(pl.multiple_of(cb * rows + r0, SUBLANES), SUBLANES), :] = h
            return carry_f, h[0:1]

        return lax.fori_loop(0, n_groups, group, carries, unroll=LRU_UNROLL)

    carry_f, carry_b = lax.fori_loop(0, n_chunks, step, (initial(0), initial(1)))
    if emit_state:
        st_ref[0, 0] = carry_f
        st_ref[0, 1] = carry_b

    def finish(c, carry):
        sl = pl.ds(pl.multiple_of(c * rows, rows), rows)
        y_ref[0, sl, :] = ((hf_ref[sl, :] + hb_ref[sl, :]) * gate_ref[0, sl, :]).astype(BF16)
        return carry

    lax.fori_loop(0, n_chunks, finish, 0)


def _lru_core(main, wa, ba, wi, bi, lam, h0, *, batch, seq, emit_state):
    nb, bs = wa.shape[1], wa.shape[2]
    width = nb * bs
    rows = min(seq, 256)
    main = main.reshape(batch, seq, -1)
    vec = lambda p: p.astype(F32).reshape(2, 1, width)
    in_specs = [
        pl.BlockSpec((1, seq, bs), lambda b, n: (b, 0, n)),
        pl.BlockSpec((1, seq, bs), lambda b, n: (b, 0, nb + n)),
        pl.BlockSpec((2, 1, bs, bs), lambda b, n: (0, n, 0, 0)),
        pl.BlockSpec((2, 1, bs, bs), lambda b, n: (0, n, 0, 0)),
        pl.BlockSpec((2, 1, bs), lambda b, n: (0, 0, n)),
        pl.BlockSpec((2, 1, bs), lambda b, n: (0, 0, n)),
        pl.BlockSpec((2, 1, bs), lambda b, n: (0, 0, n)),
    ]
    args = [main, main, wa, wi, vec(ba), vec(bi), vec(lam)]
    state_spec = pl.BlockSpec((1, 2, 1, bs), lambda b, n: (b, 0, 0, n))
    if h0 is not None:
        in_specs.append(state_spec)
        args.append(h0.astype(F32).reshape(batch, 2, 1, width))
    out_specs = [pl.BlockSpec((1, seq, bs), lambda b, n: (b, 0, n))]
    out_shape = [jax.ShapeDtypeStruct((batch, seq, width), BF16)]
    if emit_state:
        out_specs.append(state_spec)
        out_shape.append(jax.ShapeDtypeStruct((batch, 2, 1, width), F32))
    outs = pl.pallas_call(
        functools.partial(_lru_kernel, seq=seq, rows=rows, has_h0=h0 is not None, emit_state=emit_state),
        grid=(batch, nb),
        in_specs=in_specs,
        out_specs=out_specs,
        out_shape=out_shape,
        scratch_shapes=[pltpu.VMEM((seq, bs), F32), pltpu.VMEM((seq, bs), F32)]
                       + [pltpu.VMEM((rows, bs), F32) for _ in range(4)],
        compiler_params=_cparams("parallel", "parallel"),
        name="lru_core",
    )(*args)
    return (outs[0], outs[1].reshape(batch, 2, width)) if emit_state else (outs[0], None)


def _to_col_major(h):
    b, l, d = h.shape
    return h.reshape(b, l // GRID_W, GRID_W, d).transpose(0, 2, 1, 3).reshape(b, l, d)


def _to_row_major(h):
    b, l, d = h.shape
    return h.reshape(b, GRID_W, l // GRID_W, d).transpose(0, 2, 1, 3).reshape(b, l, d)


def _conv_over(n_plain, conv_w, conv_b):
    cw = jnp.concatenate([jnp.zeros((conv_w.shape[0], n_plain), F32), conv_w.astype(F32)], axis=1)
    cb = jnp.concatenate([jnp.zeros((n_plain,), F32), conv_b.astype(F32)])
    return cw, cb


def _trunk(x, mod, grid, states, p, w):
    batch, seq, _ = x.shape
    depth = mod.shape[0]
    emit = states is None
    tn = 512
    finals = dict(ssd=[], ml_c=[], ml_n=[], ml_m=[], lru=[])
    for l in range(depth):
        sh1, sc1, g1, sh2, sc2, g2 = (mod[l, :, k][:, None, :] for k in range(6))
        kind, j = l % N_MIXERS, l // N_MIXERS
        col = grid and j % 2 == 1
        xin = _to_col_major(x) if col else x
        if kind == 0:
            inner = w["ssd_out"][j].shape[0]
            n_main = w["ssd_in"].shape[2] - LANES
            cw, cb = _conv_over(inner, p["ssd_conv_w"][j], p["ssd_conv_b"][j])
            kinds = ((inner // tn, "silu"), ((n_main - inner) // tn, "conv_silu"))
            main, small = _in_proj(xin, p["norm_mix_pre"][l], sc1, sh1, w["ssd_in"], w["ssd_in_dt"][j],
                                   layer=j, n=n_main, kinds=kinds, conv_w=cw, conv_b=cb, tn=tn)
            y, s = _ssd_core(main, small, p["ssd_dt_bias"][j], p["ssd_a_log"][j], p["ssd_d"][j], p["ssd_norm_w"][j],
                             None if emit else states["ssd"][:, j], batch=batch, seq=seq, emit_state=emit)
            finals["ssd"].append(s)
            w_out = w["ssd_out"][j]
        elif kind == 1:
            d_out = w["ml_out"][j].shape[0]
            n_main = w["ml_in"].shape[2] - w["ml_n_gates"]
            kinds = (((n_main - d_out) // tn, "none"), (d_out // tn, "sigmoid"))
            main, small = _in_proj(xin, p["norm_mix_pre"][l], sc1, sh1, w["ml_in"], w["ml_in_gates"][j],
                                   layer=j, n=n_main, kinds=kinds, tn=tn)
            st = None if emit else (states["ml_c"][:, j], states["ml_n"][:, j], states["ml_m"][:, j])
            y, s = _mlstm_core(main, small, p["ml_igate_b"][j], p["ml_fgate_b"][j], p["ml_norm_w"][j], st,
                               batch=batch, seq=seq, emit_state=emit)
            if emit:
                finals["ml_c"].append(s[0])
                finals["ml_n"].append(s[1])
                finals["ml_m"].append(s[2])
            w_out = w["ml_out"][j]
        else:
            width = w["lru_out"][j].shape[0]
            cw, cb = _conv_over(width, p["lru_conv_w"][j], p["lru_conv_b"][j])
            kinds = ((width // tn, "gelu"), (width // tn, "conv_none"))
            main, _ = _in_proj(xin, p["norm_mix_pre"][l], sc1, sh1, w["lru_in"], layer=j, n=2 * width, kinds=kinds,
                               conv_w=cw, conv_b=cb, tn=tn)
            y, s = _lru_core(main, w["lru_wa"][j], p["lru_ba"][j], w["lru_wi"][j], p["lru_bi"][j], p["lru_lambda"][j],
                             None if emit else states["lru"][:, j], batch=batch, seq=seq, emit_state=emit)
            finals["lru"].append(s)
            w_out = w["lru_out"][j]
        xo = _out_proj(y, w_out, xin, g1, p["norm_mix_post"][l])
        x = _to_row_major(xo) if col else xo
        x = _mlp(x, p["norm_mlp_pre"][l], sc2, sh2, w["mlp_w1"][l], w["mlp_w2"][l], g2, p["norm_mlp_post"][l])
    if emit:
        return x, {k: jnp.stack(v, axis=1) for k, v in finals.items()}
    return x, None


def kernel(x_prompt, x_sample, state_ssd, state_mlstm_C, state_mlstm_n, state_mlstm_m, state_rglru, c, c_ctx, mod_w, mod_b, norm_mix_pre, norm_mix_post, norm_mlp_pre, norm_mlp_post, mlp_w1, mlp_w2, ssd_in_w, ssd_conv_w, ssd_conv_b, ssd_dt_bias, ssd_a_log, ssd_d, ssd_norm_w, ssd_out_w, ml_in_w, ml_igate_b, ml_fgate_b, ml_norm_w, ml_out_w, lru_in_w, lru_conv_w, lru_conv_b, lru_wa, lru_ba, lru_wi, lru_bi, lru_lambda, lru_out_w):
    p = dict(norm_mix_pre=norm_mix_pre, norm_mix_post=norm_mix_post, norm_mlp_pre=norm_mlp_pre,
             norm_mlp_post=norm_mlp_post, ssd_conv_w=ssd_conv_w, ssd_conv_b=ssd_conv_b, ssd_dt_bias=ssd_dt_bias,
             ssd_a_log=ssd_a_log, ssd_d=ssd_d, ssd_norm_w=ssd_norm_w, ml_igate_b=ml_igate_b, ml_fgate_b=ml_fgate_b,
             ml_norm_w=ml_norm_w, lru_conv_w=lru_conv_w, lru_conv_b=lru_conv_b, lru_ba=lru_ba, lru_bi=lru_bi,
             lru_lambda=lru_lambda)
    d_model = x_prompt.shape[-1]
    depth = mod_w.shape[0]
    ssd_main = ssd_in_w.shape[2] - LANES
    ml_main = ml_in_w.shape[2] - 4 * state_mlstm_m.shape[-1]
    bf = lambda a: a.astype(BF16)
    n_ml, ml_heads = ml_in_w.shape[0], state_mlstm_m.shape[-1]
    ml_gates = ml_in_w[:, :, ml_main:].reshape(n_ml, d_model, 4, ml_heads).transpose(0, 3, 2, 1)
    ml_gates = jnp.pad(ml_gates, ((0, 0), (0, 0), (0, 4), (0, 0))).reshape(n_ml, 8 * ml_heads, d_model)
    ml_gates = jnp.pad(ml_gates, ((0, 0), (0, LANES - 8 * ml_heads), (0, 0)))
    w = dict(
        mlp_w1=bf(mlp_w1), mlp_w2=bf(mlp_w2),
        ssd_in=ssd_in_w, ssd_in_dt=bf(ssd_in_w[:, :, ssd_main:].transpose(0, 2, 1)), ssd_out=bf(ssd_out_w),
        ml_in=ml_in_w, ml_in_gates=bf(ml_gates), ml_n_gates=4 * ml_heads, ml_out=bf(ml_out_w),
        lru_in=lru_in_w, lru_out=bf(lru_out_w), lru_wa=bf(lru_wa), lru_wi=bf(lru_wi),
    )
    n_dec = c.shape[0]
    n_rows = -(-(1 + n_dec) // SUBLANES) * SUBLANES
    cond = jnp.concatenate([c_ctx[None].astype(F32), c.astype(F32),
                            jnp.zeros((n_rows - 1 - n_dec, d_model), F32)], axis=0)
    mod = _mod_all(cond, mod_w, mod_b).reshape(depth, n_rows, 6, d_model)
    y_prompt, fin = _trunk(x_prompt, mod[:, 0:1], False, None, p, w)
    states = dict(ssd=state_ssd, ml_c=state_mlstm_C, ml_n=state_mlstm_n, ml_m=state_mlstm_m, lru=state_rglru)
    y_sample, _ = _trunk(x_sample, mod[:, 1:1 + n_dec], True, states, p, w)
    dt = x_prompt.dtype
    return (y_prompt, y_sample, fin["ssd"].astype(dt), fin["ml_c"].astype(dt), fin["ml_n"].astype(dt),
            fin["ml_m"].astype(dt), fin["lru"].astype(dt))
```

```python
import functools

import jax
import jax.numpy as jnp
from jax import lax
from jax.experimental import pallas as pl
from jax.experimental.pallas import tpu as pltpu

F32 = jnp.float32
BF16 = jnp.bfloat16
HIGHEST = lax.Precision.HIGHEST

EPS = 1e-6
CHUNK = 128
GRID_W = 64
N_MIXERS = 3
LRU_C = 8.0
SUBLANES = 8
LANES = 128
HALO = 16
CONV_ROWS = 256
VMEM_LIMIT_BYTES = 56 * 1024 * 1024


def _cparams(*semantics):
    return pltpu.CompilerParams(dimension_semantics=semantics, vmem_limit_bytes=VMEM_LIMIT_BYTES)


def _dot(a, b):
    return jnp.dot(a.astype(BF16), b.astype(BF16), preferred_element_type=F32)


def _dot_exact(a, b):
    return jnp.dot(a, b, precision=HIGHEST, preferred_element_type=F32)


def _softplus(x):
    return jnp.maximum(x, 0.0) + jnp.log1p(jnp.exp(-jnp.abs(x)))


def _sigmoid(x):
    return jax.nn.sigmoid(x)


def _silu(x):
    return x * jax.nn.sigmoid(x)


def _rms(x, w):
    return x * lax.rsqrt(jnp.mean(x * x, axis=-1, keepdims=True) + EPS) * w


def _tri(n, lower):
    r = lax.broadcasted_iota(jnp.int32, (n, n), 0)
    c = lax.broadcasted_iota(jnp.int32, (n, n), 1)
    return (r >= c if lower else r <= c).astype(F32)


def _split3(v):
    hi = v.astype(BF16).astype(F32)
    r = v - hi
    mid = r.astype(BF16).astype(F32)
    return hi, mid, r - mid


_ACTS = {
    "none": lambda v: v,
    "silu": _silu,
    "sigmoid": _sigmoid,
    "gelu": lambda v: jax.nn.gelu(v, approximate=True),
}


def _mod_kernel(c_ref, w_ref, b_ref, o_ref):
    a = _silu(c_ref[...])
    o_ref[0] = _dot(a, w_ref[0]) + b_ref[0]


def _mod_all(cond, mod_w, mod_b):
    depth, d, n = mod_w.shape
    r = cond.shape[0]
    tn = 1024
    return pl.pallas_call(
        _mod_kernel,
        grid=(depth, n // tn),
        in_specs=[
            pl.BlockSpec((r, d), lambda l, j: (0, 0)),
            pl.BlockSpec((1, d, tn), lambda l, j: (l, 0, j)),
            pl.BlockSpec((1, 1, tn), lambda l, j: (l, 0, j)),
        ],
        out_specs=pl.BlockSpec((1, r, tn), lambda l, j: (l, 0, j)),
        out_shape=jax.ShapeDtypeStruct((depth, r, n), F32),
        compiler_params=_cparams("parallel", "parallel"),
        name="mod_map",
    )(cond, mod_w, mod_b.reshape(depth, 1, n))


def _in_proj_kernel(*refs, kinds, seq, tm, halo, has_conv, has_small):
    refs = list(refs)
    x_ref = refs.pop(0)
    xp_ref, xn_ref = (refs.pop(0), refs.pop(0)) if halo else (None, None)
    nw_ref, sc_ref, sh_ref, w_ref = refs.pop(0), refs.pop(0), refs.pop(0), refs.pop(0)
    cw_ref, cb_ref = (refs.pop(0), refs.pop(0)) if has_conv else (None, None)
    ws_ref = refs.pop(0) if has_small else None
    o_ref = refs.pop(0)
    os_ref = refs.pop(0) if has_small else None
    (h_ref,) = refs
    i, j = pl.program_id(0), pl.program_id(1)
    pad = HALO if halo else 0
    main = pl.ds(pad, tm)

    @pl.when(j == 0)
    def _():
        norm = lambda v: (_rms(v, nw_ref[...]) * (1.0 + sc_ref[0]) + sh_ref[0]).astype(BF16)
        h_ref[main, :] = norm(x_ref[...])
        if halo:
            h_ref[pl.ds(0, pad), :] = norm(xp_ref[...])
            h_ref[pl.ds(pad + tm, pad), :] = norm(xn_ref[...])
        if has_small:
            os_ref[...] = lax.dot_general(ws_ref[...], h_ref[main, :], (((1,), (1,)), ((), ())),
                                          preferred_element_type=F32)

    def plain(act):
        o_ref[...] = act(jnp.dot(h_ref[main, :], w_ref[0].astype(BF16), preferred_element_type=F32))

    def conv(act):
        w = cw_ref[...]
        wm = w_ref[0].astype(BF16)
        rb = min(CONV_ROWS, tm) if halo else min(tm, max(CONV_ROWS, seq))
        for blk in range(tm // rb):
            lo = blk * rb
            r = jnp.dot(h_ref[pl.ds(lo, rb + 2 * pad), :], wm, preferred_element_type=F32)
            n = r.shape[0]
            if halo:
                lead, tail = r[:pad], r[pad + rb:]
                if blk == 0:
                    lead = jnp.where((i * tm) % seq == 0, 0.0, lead)
                if blk == tm // rb - 1:
                    tail = jnp.where(((i + 1) * tm) % seq == 0, 0.0, tail)
                r = jnp.concatenate([lead, r[pad:pad + rb], tail], axis=0)
            tap = lambda shift: pltpu.roll(r, shift % n, 0)[pad:pad + rb]
            t2, t1, tn1 = tap(2), tap(1), tap(-1)
            if not halo:
                pos = (lo + lax.broadcasted_iota(jnp.int32, r.shape, 0)) & (seq - 1)
                t2 = jnp.where(pos >= 2, t2, 0.0)
                t1 = jnp.where(pos >= 1, t1, 0.0)
                tn1 = jnp.where(pos < seq - 1, tn1, 0.0)
            o_ref[pl.ds(lo, rb), :] = act(w[0:1] * t2 + w[1:2] * t1 + w[2:3] * r[pad:pad + rb] + w[3:4] * tn1
                                          + cb_ref[...])

    start = 0
    for count, kind in kinds:
        body = (functools.partial(conv, _ACTS[kind[5:]]) if kind.startswith("conv_")
                else functools.partial(plain, _ACTS[kind]))
        pl.when((j >= start) & (j < start + count))(body)
        start += count


def _in_proj(x, norm_w, scale, shift, w_all, w_small=None, *, layer, n, kinds, conv_w=None, conv_b=None,
             tm=1024, tn=512):
    b, l, d = x.shape
    m = b * l
    per_batch = scale.shape[0] > 1
    assert m % tm == 0 and n % tn == 0 and (not per_batch or l % tm == 0)
    assert sum(c for c, _ in kinds) == n // tn
    has_conv = any(k.startswith("conv_") for _, k in kinds)
    halo = has_conv and l > tm
    assert not has_conv or (l % tm == 0 if halo else (tm % l == 0 and l & (l - 1) == 0))
    mod_idx = (lambda i, j: ((i * tm) // l, 0, 0)) if per_batch else (lambda i, j: (0, 0, 0))
    has_small = w_small is not None
    x2 = x.reshape(m, d)
    in_specs = [pl.BlockSpec((tm, d), lambda i, j: (i, 0))]
    args = [x2]
    if halo:
        per_tile, last = tm // HALO, m // HALO - 1
        in_specs += [pl.BlockSpec((HALO, d), lambda i, j: (jnp.maximum(i * per_tile - 1, 0), 0)),
                     pl.BlockSpec((HALO, d), lambda i, j: (jnp.minimum((i + 1) * per_tile, last), 0))]
        args += [x2, x2]
    in_specs += [
        pl.BlockSpec((1, d), lambda i, j: (0, 0)),
        pl.BlockSpec((1, 1, d), mod_idx),
        pl.BlockSpec((1, 1, d), mod_idx),
        pl.BlockSpec((1, d, tn), lambda i, j: (layer, 0, j)),
    ]
    args += [norm_w.reshape(1, d), scale, shift, w_all]
    if has_conv:
        in_specs += [pl.BlockSpec((4, tn), lambda i, j: (0, j)), pl.BlockSpec((1, tn), lambda i, j: (0, j))]
        args += [conv_w.astype(F32), conv_b.astype(F32).reshape(1, n)]
    out_specs = [pl.BlockSpec((tm, tn), lambda i, j: (i, j))]
    out_shape = [jax.ShapeDtypeStruct((m, n), F32)]
    if has_small:
        ns = w_small.shape[0]
        in_specs.append(pl.BlockSpec((ns, d), lambda i, j: (0, 0)))
        out_specs.append(pl.BlockSpec((ns, tm), lambda i, j: (0, i)))
        out_shape.append(jax.ShapeDtypeStruct((ns, m), F32))
        args.append(w_small)
    outs = pl.pallas_call(
        functools.partial(_in_proj_kernel, kinds=tuple(kinds), seq=l, tm=tm, halo=halo, has_conv=has_conv,
                          has_small=has_small),
        grid=(m // tm, n // tn),
        in_specs=in_specs,
        out_specs=out_specs,
        out_shape=out_shape,
        scratch_shapes=[pltpu.VMEM((tm + (2 * HALO if halo else 0), d), BF16)],
        compiler_params=_cparams("parallel", "arbitrary"),
        name="in_proj",
    )(*args)
    return (outs[0], outs[1]) if has_small else (outs[0], None)


def _out_proj_kernel(y_ref, w_ref, x_ref, g_ref, nw_ref, o_ref, *acc, nk):
    k = pl.program_id(1)
    part = jnp.dot(y_ref[...], w_ref[...], preferred_element_type=F32)
    finish = lambda f: x_ref[...] + g_ref[0] * _rms(f, nw_ref[...])
    if nk == 1:
        o_ref[...] = finish(part)
        return
    (acc_ref,) = acc

    @pl.when(k == 0)
    def _():
        acc_ref[...] = part

    @pl.when((k > 0) & (k < nk - 1))
    def _():
        acc_ref[...] += part

    @pl.when(k == nk - 1)
    def _():
        o_ref[...] = finish(acc_ref[...] + part)


def _out_proj(y, w, x, gate, norm_w, *, tm=512, tk=2048):
    b, l, d = x.shape
    m = b * l
    kin = w.shape[0]
    per_batch = gate.shape[0] > 1
    assert m % tm == 0 and kin % tk == 0 and (not per_batch or l % tm == 0)
    mod_idx = (lambda i, k: ((i * tm) // l, 0, 0)) if per_batch else (lambda i, k: (0, 0, 0))
    nk = kin // tk
    out = pl.pallas_call(
        functools.partial(_out_proj_kernel, nk=nk),
        grid=(m // tm, nk),
        in_specs=[
            pl.BlockSpec((tm, tk), lambda i, k: (i, k)),
            pl.BlockSpec((tk, d), lambda i, k: (k, 0)),
            pl.BlockSpec((tm, d), lambda i, k: (i, 0)),
            pl.BlockSpec((1, 1, d), mod_idx),
            pl.BlockSpec((1, d), lambda i, k: (0, 0)),
        ],
        out_specs=pl.BlockSpec((tm, d), lambda i, k: (i, 0)),
        out_shape=jax.ShapeDtypeStruct((m, d), F32),
        scratch_shapes=[pltpu.VMEM((tm, d), F32)] if nk > 1 else [],
        compiler_params=_cparams("parallel", "arbitrary"),
        name="out_proj",
    )(y.reshape(m, kin), w, x.reshape(m, d), gate, norm_w.reshape(1, d))
    return out.reshape(b, l, d)


def _mlp_kernel(x_ref, nw_ref, sc_ref, sh_ref, w1_ref, w2_ref, g_ref, pw_ref, o_ref, h_ref, acc_ref, *, nf):
    j = pl.program_id(1)

    @pl.when(j == 0)
    def _():
        h = _rms(x_ref[...], nw_ref[...]) * (1.0 + sc_ref[0]) + sh_ref[0]
        h_ref[...] = h.astype(BF16)
        acc_ref[...] = jnp.zeros_like(acc_ref)

    t = jnp.dot(h_ref[...], w1_ref[...], preferred_element_type=F32)
    t = jnp.square(jnp.maximum(t, 0.0))
    acc_ref[...] += jnp.dot(t.astype(BF16), w2_ref[...], preferred_element_type=F32)

    @pl.when(j == nf - 1)
    def _():
        o_ref[...] = x_ref[...] + g_ref[0] * _rms(acc_ref[...], pw_ref[...])


def _mlp(x, pre_w, scale, shift, w1, w2, gate, post_w, *, tm=512, tf=1024):
    b, l, d = x.shape
    m = b * l
    dff = w1.shape[1]
    per_batch = scale.shape[0] > 1
    assert m % tm == 0 and dff % tf == 0 and (not per_batch or l % tm == 0)
    mod_idx = (lambda i, j: ((i * tm) // l, 0, 0)) if per_batch else (lambda i, j: (0, 0, 0))
    nf = dff // tf
    out = pl.pallas_call(
        functools.partial(_mlp_kernel, nf=nf),
        grid=(m // tm, nf),
        in_specs=[
            pl.BlockSpec((tm, d), lambda i, j: (i, 0)),
            pl.BlockSpec((1, d), lambda i, j: (0, 0)),
            pl.BlockSpec((1, 1, d), mod_idx),
            pl.BlockSpec((1, 1, d), mod_idx),
            pl.BlockSpec((d, tf), lambda i, j: (0, j)),
            pl.BlockSpec((tf, d), lambda i, j: (j, 0)),
            pl.BlockSpec((1, 1, d), mod_idx),
            pl.BlockSpec((1, d), lambda i, j: (0, 0)),
        ],
        out_specs=pl.BlockSpec((tm, d), lambda i, j: (i, 0)),
        out_shape=jax.ShapeDtypeStruct((m, d), F32),
        scratch_shapes=[pltpu.VMEM((tm, d), BF16), pltpu.VMEM((tm, d), F32)],
        compiler_params=_cparams("parallel", "arbitrary"),
        name="mlp",
    )(x.reshape(m, d), pre_w.reshape(1, d), scale, shift, w1, w2, gate, post_w.reshape(1, d))
    return out.reshape(b, l, d)


SSD_GROUP_HEADS = 8
SSD_HEAD_DIM = 64


def _ssd_kernel(*refs, seq, has_h0, emit_state):
    z_ref, x_ref, b_ref, c_ref, dtf_ref, dtb_ref, biast_ref, alogt_ref, dskip_ref, nw_ref = refs[:10]
    rest = list(refs[10:])
    h0_ref = rest.pop(0) if has_h0 else None
    y_ref = rest.pop(0)
    st_ref = rest.pop(0) if emit_state else None
    ya_ref, yt_ref, hf_ref, hb_ref, ct_ref, xt_ref, qry_ref, keyf_ref, keyb_ref, dts_ref, erow_ref, wrow_ref = rest
    n_chunks = seq // CHUNK
    gh, hd = SSD_GROUP_HEADS, SSD_HEAD_DIM
    hp = gh * hd
    wide = gh * CHUNK
    li = lax.broadcasted_iota(jnp.int32, (CHUNK, CHUNK), 0)
    si = lax.broadcasted_iota(jnp.int32, (CHUNK, CHUNK), 1)
    lane = lax.broadcasted_iota(jnp.int32, (CHUNK, LANES), 1)

    onehot = lambda cond: jnp.where(cond, 1.0, 0.0).astype(BF16)
    k_i = lax.broadcasted_iota(jnp.int32, (48, wide), 0)
    head_of_lane = lax.broadcasted_iota(jnp.int32, (48, wide), 1) >> 7
    pick_f = onehot((k_i & 15) == head_of_lane)
    pick_b = onehot((k_i & 15) == gh + head_of_lane)
    zero_tail = jnp.zeros((48, wide), BF16)
    diag_blocks = (lax.broadcasted_iota(jnp.int32, (gh, wide), 0)
                   == lax.broadcasted_iota(jnp.int32, (gh, wide), 1) >> 7)

    def per_head(t, rows):
        return jnp.concatenate([t[hd * h:hd * (h + 1)] * rows[h:h + 1] for h in range(gh)], axis=0)

    def key_rows(parts, d):
        blocks = [jnp.where(diag_blocks, -jnp.tile(p[d:d + gh], (1, gh)), 0.0) for p in parts]
        blocks.append(jnp.zeros((gh, wide), F32))
        return jnp.concatenate(blocks, axis=0).astype(BF16)

    dt_all = _softplus(jnp.concatenate([dtf_ref[...], dtb_ref[...]], axis=0) + biast_ref[0])
    dts_ref[...] = dt_all
    stack = lambda v: jnp.concatenate([v[:, CHUNK * c:CHUNK * (c + 1)] for c in range(n_chunks)], axis=0)
    dt_st = stack(dt_all)
    a_st = stack(dt_all * (-jnp.exp(alogt_ref[0])))
    cs = _dot_exact(a_st, _tri(CHUNK, False))
    tot = cs[:, CHUNK - 1:CHUNK]
    backward = (lax.broadcasted_iota(jnp.int32, cs.shape, 0) & (2 * gh - 1)) >= gh
    pos = jnp.where(backward, tot - cs + a_st, cs)
    erow_ref[...] = jnp.exp(pos)
    wrow_ref[...] = dt_st * jnp.exp(tot - pos)
    pos3 = _split3(pos)
    for c in range(n_chunks):
        blk, tok = slice(2 * gh * c, 2 * gh * (c + 1)), slice(CHUNK * c, CHUNK * (c + 1))
        parts = [p[blk] for p in pos3]
        qry_ref[tok, :] = jnp.concatenate([*parts, jnp.ones((32, CHUNK), F32), jnp.zeros((48, CHUNK), F32)],
                                          axis=0).T.astype(BF16)
        keyf_ref[32 * c:32 * (c + 1), :] = key_rows(parts, 0)
        keyb_ref[32 * c:32 * (c + 1), :] = key_rows(parts, gh)

    def within(c, carry):
        t0 = pl.multiple_of(c * CHUNK, CHUNK)
        rows = pl.ds(t0, CHUNK)
        x = x_ref[0, rows, :]
        cm = c_ref[0, rows, :]
        bt = b_ref[0, rows, :].T
        ct_ref[rows, :] = cm.T.astype(BF16)
        xt_ref[pl.ds(pl.multiple_of(c * hp, hp), hp), :] = x.T
        dtt = dts_ref[:, rows]
        query = qry_ref[rows, :]
        keys = pl.ds(pl.multiple_of(c * 32, 32), 32)
        arg_f = jnp.dot(query, jnp.concatenate([pick_f, keyf_ref[keys, :], zero_tail], axis=0),
                        preferred_element_type=F32)
        arg_b = jnp.dot(query, jnp.concatenate([pick_b, keyb_ref[keys, :], zero_tail], axis=0),
                        preferred_element_type=F32)
        cb = _dot(cm, bt)
        ydiag = []
        for p in range(gh // 2):
            wms = []
            for h in (2 * p, 2 * p + 1):
                arg = jnp.where(si <= li, arg_f[:, CHUNK * h:CHUNK * (h + 1)], arg_b[:, CHUNK * h:CHUNK * (h + 1)])
                coef = jnp.where(si < li, dtt[h:h + 1, :],
                                 jnp.where(si > li, dtt[gh + h:gh + h + 1, :], dtt[h:h + 1, :] + dtt[gh + h:gh + h + 1, :]))
                wms.append(cb * jnp.exp(arg) * coef)
            xp = x[:, LANES * p:LANES * (p + 1)]
            rhs = jnp.concatenate([jnp.where(lane < hd, xp, 0.0), jnp.where(lane >= hd, xp, 0.0)], axis=0)
            ydiag.append(_dot(jnp.concatenate(wms, axis=1), rhs))
        ya_ref[rows, :] = jnp.concatenate(ydiag, axis=1)
        return carry

    lax.fori_loop(0, n_chunks, within, 0, unroll=min(4, n_chunks))

    for d, st in ((0, hf_ref), (1, hb_ref)):
        st[...] = h0_ref[0, d].reshape(hp, -1) if has_h0 else jnp.zeros_like(st)
    yt_ref[...] = jnp.zeros_like(yt_ref)

    def carried(c, d, st, edge):
        rows = pl.ds(pl.multiple_of(c * CHUNK, CHUNK), CHUNK)
        chans = pl.ds(pl.multiple_of(c * hp, hp), hp)
        heads = pl.ds(pl.multiple_of(c * 2 * gh + d * gh, gh), gh)
        decay, weight = erow_ref[heads, :], wrow_ref[heads, :]
        h = st[...]
        yt_ref[chans, :] += per_head(_dot(h, ct_ref[rows, :]), decay)
        inc = _dot(per_head(xt_ref[chans, :], weight), b_ref[0, rows, :])
        st[...] = per_head(h, decay[:, edge:edge + 1]) + inc

    def across(i, carry):
        carried(i, 0, hf_ref, CHUNK - 1)
        carried(n_chunks - 1 - i, 1, hb_ref, 0)
        return carry

    lax.fori_loop(0, n_chunks, across, 0)
    if emit_state:
        for d, st in ((0, hf_ref), (1, hb_ref)):
            st_ref[0, d] = st[...].reshape(st_ref.shape[2:])

    def finish(c, carry):
        rows = pl.ds(pl.multiple_of(c * CHUNK, CHUNK), CHUNK)
        y = ya_ref[rows, :] + yt_ref[pl.ds(pl.multiple_of(c * hp, hp), hp), :].T
        y = (y + dskip_ref[...] * x_ref[0, rows, :]) * z_ref[0, rows, :]
        y_ref[0, rows, :] = _rms(y, nw_ref[...]).astype(BF16)
        return carry

    lax.fori_loop(0, n_chunks, finish, 0, unroll=2)


def _ssd_core(main, small, dt_bias, a_log, d_skip, norm_w, h0, *, batch, seq, emit_state):
    heads, groups, hd, hg = 64, 8, SSD_HEAD_DIM, SSD_GROUP_HEADS
    nstate = 128
    inner = heads * hd
    hp = inner // groups
    main = main.reshape(batch, seq, -1)

    def per_group(p):
        return p.astype(F32).reshape(2, groups, hg).transpose(1, 0, 2).reshape(groups, 2 * hg, 1)

    dsk = jnp.repeat(d_skip.astype(F32), hd).reshape(1, inner)
    nw = norm_w.astype(F32).reshape(1, inner)
    xo, bo, co = inner // hp, 2 * inner // nstate, (2 * inner + groups * nstate) // nstate
    in_specs = [
        pl.BlockSpec((1, seq, hp), lambda b, g: (b, 0, g)),
        pl.BlockSpec((1, seq, hp), lambda b, g: (b, 0, xo + g)),
        pl.BlockSpec((1, seq, nstate), lambda b, g: (b, 0, bo + g)),
        pl.BlockSpec((1, seq, nstate), lambda b, g: (b, 0, co + g)),
        pl.BlockSpec((hg, seq), lambda b, g: (g, b)),
        pl.BlockSpec((hg, seq), lambda b, g: (groups + g, b)),
        pl.BlockSpec((1, 2 * hg, 1), lambda b, g: (g, 0, 0)),
        pl.BlockSpec((1, 2 * hg, 1), lambda b, g: (g, 0, 0)),
        pl.BlockSpec((1, hp), lambda b, g: (0, g)),
        pl.BlockSpec((1, hp), lambda b, g: (0, g)),
    ]
    args = [main, main, main, main, small, small, per_group(dt_bias), per_group(a_log), dsk, nw]
    state_spec = pl.BlockSpec((1, 2, hg, hd, nstate), lambda b, g: (b, 0, g, 0, 0))
    if h0 is not None:
        in_specs.append(state_spec)
        args.append(h0.astype(F32))
    out_specs = [pl.BlockSpec((1, seq, hp), lambda b, g: (b, 0, g))]
    out_shape = [jax.ShapeDtypeStruct((batch, seq, inner), BF16)]
    if emit_state:
        out_specs.append(state_spec)
        out_shape.append(jax.ShapeDtypeStruct((batch, 2, heads, hd, nstate), F32))
    outs = pl.pallas_call(
        functools.partial(_ssd_kernel, seq=seq, has_h0=h0 is not None, emit_state=emit_state),
        grid=(batch, groups),
        in_specs=in_specs,
        out_specs=out_specs,
        out_shape=out_shape,
        scratch_shapes=[pltpu.VMEM((seq, hp), F32), pltpu.VMEM((seq // CHUNK * hp, CHUNK), F32),
                        pltpu.VMEM((hp, nstate), F32), pltpu.VMEM((hp, nstate), F32),
                        pltpu.VMEM((seq // CHUNK * nstate, CHUNK), BF16), pltpu.VMEM((seq // CHUNK * hp, CHUNK), F32),
                        pltpu.VMEM((seq, LANES), BF16),
                        pltpu.VMEM((seq // CHUNK * 32, hg * CHUNK), BF16), pltpu.VMEM((seq // CHUNK * 32, hg * CHUNK), BF16),
                        pltpu.VMEM((2 * hg, seq), F32),
                        pltpu.VMEM((seq // CHUNK * 2 * hg, CHUNK), F32), pltpu.VMEM((seq // CHUNK * 2 * hg, CHUNK), F32)],
        compiler_params=_cparams("parallel", "parallel"),
        name="ssd_core",
    )(*args)
    return (outs[0], outs[1]) if emit_state else (outs[0], None)


def _mlstm_kernel(*refs, seq, has_state, emit_state):
    q_ref, k_ref, v_ref, o_ref, gr_ref, br_ref, nw_ref = refs[:7]
    rest = list(refs[7:])
    if has_state:
        c0_ref, n0_ref, m0_ref = rest.pop(0), rest.pop(0), rest.pop(0)
    y_ref = rest.pop(0)
    if emit_state:
        cf_ref, nf_ref, mf_ref = rest.pop(0), rest.pop(0), rest.pop(0)
    sv_ref, rsum_ref, cmxb_ref, accb_ref, u_ref, rows_ref, cmx_ref, cols_ref, caf_ref, cab_ref = rest
    n_chunks = seq // CHUNK
    dqk = q_ref.shape[2]
    dv = v_ref.shape[2]
    li = lax.broadcasted_iota(jnp.int32, (CHUNK, CHUNK), 0)
    si = lax.broadcasted_iota(jnp.int32, (CHUNK, CHUNK), 1)
    ones_blk = jnp.ones((CHUNK, LANES), F32)
    k_scale = dqk ** -0.5

    def load_state(ca_ref, d):
        if has_state:
            ca_ref[...] = jnp.concatenate([c0_ref[0, d, 0], jnp.broadcast_to(n0_ref[0, d, 0], (dqk, LANES))], axis=1)
            return m0_ref[0, d, 0]
        ca_ref[...] = jnp.zeros_like(ca_ref)
        return jnp.zeros((1, 1), F32)

    def store_state(ca_ref, d, m):
        if emit_state:
            cf_ref[0, d, 0] = ca_ref[:, :dv]
            nf_ref[0, d, 0] = ca_ref[:, dv:dv + 1]
            mf_ref[0, d, 0] = m


    g_all = gr_ref[...] + br_ref[0]
    g_st = jnp.concatenate([g_all[:, CHUNK * c:CHUNK * (c + 1)] for c in range(n_chunks)], axis=0)
    r8 = lax.broadcasted_iota(jnp.int32, g_st.shape, 0) & 7
    tok = lax.broadcasted_iota(jnp.int32, g_st.shape, 1)
    g_st = jnp.where((r8 == 1) | (r8 == 3), -_softplus(-g_st), g_st)
    cum = _dot_exact(g_st, _tri(CHUNK, False))
    tot = cum[:, CHUNK - 1:CHUNK]
    acc = jnp.where(r8 == 1, cum, jnp.where(r8 == 3, tot - cum + g_st, 0.0))
    neg = g_st - pltpu.roll(acc, acc.shape[0] - 1, 0)
    pmax, smax = neg, neg
    for k in (1, 2, 4, 8, 16, 32, 64):
        pmax = jnp.where(tok >= k, jnp.maximum(pmax, pltpu.roll(pmax, k, 1)), pmax)
        smax = jnp.where(tok < CHUNK - k, jnp.maximum(smax, pltpu.roll(smax, CHUNK - k, 1)), smax)
    cmx = jnp.where(r8 == 0, pmax, jnp.where(r8 == 2, smax, 0.0))
    terms_all = jnp.where((r8 == 0) | (r8 == 2), neg, acc)
    rows_ref[...] = terms_all
    cmx_ref[...] = cmx
    for c in range(n_chunks):
        blk = slice(8 * c, 8 * (c + 1))
        cols_ref[CHUNK * c:CHUNK * (c + 1), :] = jnp.concatenate(
            [terms_all[blk], cmx[blk], jnp.zeros((CHUNK - 16, CHUNK), F32)], axis=0).T

    def within(c, carry):
        rows = pl.ds(pl.multiple_of(c * CHUNK, CHUNK), CHUNK)
        r8c = pl.ds(pl.multiple_of(c * 8, 8), 8)
        q = q_ref[0, rows, :]
        kt = (k_ref[0, rows, :] * k_scale).T
        v = v_ref[0, rows, :]
        vaug = jnp.concatenate([v, ones_blk], axis=1)
        qk = _dot(q, kt)
        terms, cmxr, cols = rows_ref[r8c, :], cmx_ref[r8c, :], cols_ref[rows, :]
        for d in (0, 1):
            neg_row = terms[2 * d:2 * d + 1]
            cmx_rep = jnp.broadcast_to(cols[:, 8 + 2 * d:9 + 2 * d], (CHUNK, LANES))
            cmxb_ref[d, rows, :] = cmx_rep
            accb_ref[d, rows, :] = jnp.broadcast_to(cols[:, 1 + 2 * d:2 + 2 * d], (CHUNK, LANES))
            s0 = jnp.where(si >= li if d else si <= li, qk * jnp.exp(neg_row - cmx_rep), 0.0)
            sv_ref[d, rows, :] = _dot(s0, v)
            rsum_ref[d, rows, :] = jnp.broadcast_to(jnp.sum(s0, axis=1, keepdims=True), (CHUNK, LANES))
            maxneg = cmxr[2:3, 0:1] if d else cmxr[0:1, CHUNK - 1:CHUNK]
            u_ref[d, pl.ds(pl.multiple_of(c * dqk, dqk), dqk), :] = _dot(kt * jnp.exp(neg_row - maxneg), vaug)
        return carry

    lax.fori_loop(0, n_chunks, within, 0, unroll=2)

    def carried(c, d, ca_ref, m):
        rows = pl.ds(pl.multiple_of(c * CHUNK, CHUNK), CHUNK)
        r8c = pl.ds(pl.multiple_of(c * 8, 8), 8)
        ca = ca_ref[...]
        qc = _dot(q_ref[0, rows, :], ca)
        cmx_rep = cmxb_ref[d, rows, :]
        big = jnp.maximum(m, cmx_rep)
        r = jnp.exp(cmx_rep - big)
        w_in = jnp.exp(m - big)
        den = r * rsum_ref[d, rows, :] + w_in * qc[:, dv:]
        inv = 1.0 / jnp.maximum(jnp.abs(den), jnp.exp(-(accb_ref[d, rows, :] + big)))
        wide = lambda t: jnp.concatenate([t] * (dv // LANES), axis=1)
        sv_ref[d, rows, :] = wide(r * inv) * sv_ref[d, rows, :] + wide(w_in * inv) * qc[:, :dv]
        terms, cmxr = rows_ref[r8c, :], cmx_ref[r8c, :]
        tot_c = terms[3:4, 0:1] if d else terms[1:2, CHUNK - 1:CHUNK]
        maxneg = cmxr[2:3, 0:1] if d else cmxr[0:1, CHUNK - 1:CHUNK]
        top = jnp.maximum(m, maxneg)
        ca_ref[...] = jnp.exp(m - top) * ca + jnp.exp(maxneg - top) * u_ref[d, pl.ds(pl.multiple_of(c * dqk, dqk), dqk), :]
        return tot_c + top

    def across(i, ms):
        return carried(i, 0, caf_ref, ms[0]), carried(n_chunks - 1 - i, 1, cab_ref, ms[1])

    m_f, m_b = lax.fori_loop(0, n_chunks, across, (load_state(caf_ref, 0), load_state(cab_ref, 1)))
    store_state(caf_ref, 0, m_f)
    store_state(cab_ref, 1, m_b)

    def finish(c, carry):
        rows = pl.ds(pl.multiple_of(c * CHUNK, CHUNK), CHUNK)
        h = _rms(sv_ref[0, rows, :] + sv_ref[1, rows, :], nw_ref[...])
        y_ref[0, rows, :] = (h * o_ref[0, rows, :]).astype(BF16)
        return carry

    lax.fori_loop(0, n_chunks, finish, 0)


def _mlstm_core(main, small, igate_b, fgate_b, norm_w, state, *, batch, seq, emit_state):
    heads, dqk, dv = 8, 128, 256
    main = main.reshape(batch, seq, -1)
    bias = jnp.stack([igate_b[0], fgate_b[0], igate_b[1], fgate_b[1]], axis=1).astype(F32)
    bias = jnp.pad(bias, ((0, 0), (0, 4)))
    in_specs = [
        pl.BlockSpec((1, seq, dqk), lambda b, h: (b, 0, h)),
        pl.BlockSpec((1, seq, dqk), lambda b, h: (b, 0, heads + h)),
        pl.BlockSpec((1, seq, dv), lambda b, h: (b, 0, heads + h)),
        pl.BlockSpec((1, seq, dv), lambda b, h: (b, 0, 2 * heads + h)),
        pl.BlockSpec((8, seq), lambda b, h: (h, b)),
        pl.BlockSpec((1, 8, 1), lambda b, h: (h, 0, 0)),
        pl.BlockSpec((1, dv), lambda b, h: (0, h)),
    ]
    args = [main, main, main, main, small, bias.reshape(heads, 8, 1), norm_w.astype(F32).reshape(1, heads * dv)]
    c_spec = pl.BlockSpec((1, 2, 1, dqk, dv), lambda b, h: (b, 0, h, 0, 0))
    n_spec = pl.BlockSpec((1, 2, 1, dqk, 1), lambda b, h: (b, 0, h, 0, 0))
    m_spec = pl.BlockSpec((1, 2, 1, 1, 1), lambda b, h: (b, 0, h, 0, 0))
    if state is not None:
        c0, n0, m0 = state
        in_specs += [c_spec, n_spec, m_spec]
        args += [c0.astype(F32), n0.astype(F32).reshape(batch, 2, heads, dqk, 1),
                 m0.astype(F32).reshape(batch, 2, heads, 1, 1)]
    out_specs = [pl.BlockSpec((1, seq, dv), lambda b, h: (b, 0, h))]
    out_shape = [jax.ShapeDtypeStruct((batch, seq, heads * dv), BF16)]
    if emit_state:
        out_specs += [c_spec, n_spec, m_spec]
        out_shape += [jax.ShapeDtypeStruct((batch, 2, heads, dqk, dv), F32),
                      jax.ShapeDtypeStruct((batch, 2, heads, dqk, 1), F32),
                      jax.ShapeDtypeStruct((batch, 2, heads, 1, 1), F32)]
    outs = pl.pallas_call(
        functools.partial(_mlstm_kernel, seq=seq, has_state=state is not None, emit_state=emit_state),
        grid=(batch, heads),
        in_specs=in_specs,
        out_specs=out_specs,
        out_shape=out_shape,
        scratch_shapes=[pltpu.VMEM((2, seq, dv), F32), pltpu.VMEM((2, seq, LANES), F32),
                        pltpu.VMEM((2, seq, LANES), F32), pltpu.VMEM((2, seq, LANES), F32),
                        pltpu.VMEM((2, seq // CHUNK * dqk, dv + LANES), F32),
                        pltpu.VMEM((seq // CHUNK * 8, CHUNK), F32), pltpu.VMEM((seq // CHUNK * 8, CHUNK), F32),
                        pltpu.VMEM((seq, LANES), F32),
                        pltpu.VMEM((dqk, dv + LANES), F32), pltpu.VMEM((dqk, dv + LANES), F32)],
        compiler_params=_cparams("parallel", "parallel"),
        name="mlstm_core",
    )(*args)
    if emit_state:
        return outs[0], (outs[1], outs[2].reshape(batch, 2, heads, dqk), outs[3].reshape(batch, 2, heads))
    return outs[0], None


LRU_UNROLL = 8


def _scan8(a, b, reverse):
    row = lax.broadcasted_iota(jnp.int32, a.shape, 0)
    for k in (1, 2, 4):
        shift = SUBLANES - k if reverse else k
        valid = row < SUBLANES - k if reverse else row >= k
        a_prev = pltpu.roll(a, shift, 0)
        b_prev = pltpu.roll(b, shift, 0)
        b = jnp.where(valid, a * b_prev, 0.0) + b
        a = jnp.where(valid, a * a_prev, a)
    return a, b


def _lru_kernel(*refs, seq, rows, has_h0, emit_state):
    gate_ref, x_ref, wa_ref, wi_ref, ba_ref, bi_ref, lam_ref = refs[:7]
    rest = list(refs[7:])
    h0_ref = rest.pop(0) if has_h0 else None
    y_ref = rest.pop(0)
    st_ref = rest.pop(0) if emit_state else None
    hf_ref, hb_ref, af_ref, bf_ref, ab_ref, bb_ref = rest
    n_chunks = seq // rows
    n_groups = rows // SUBLANES
    width = x_ref.shape[2]

    def coefficients(c, d, a_ref, b_ref):
        xc = x_ref[0, pl.ds(pl.multiple_of(c * rows, rows), rows), :]
        r = _sigmoid(_dot(xc, wa_ref[d, 0]) + ba_ref[d])
        i = _sigmoid(_dot(xc, wi_ref[d, 0]) + bi_ref[d])
        log_a = -LRU_C * r * _softplus(-lam_ref[d])
        a = jnp.exp(log_a)
        a_ref[...] = a
        b_ref[...] = jnp.sqrt(-jnp.tanh(log_a) * (1.0 + a * a)) * (i * xc)

    def initial(d):
        return h0_ref[0, d] if has_h0 else jnp.zeros((1, width), F32)

    def step(i, carries):
        cf, cb = i, n_chunks - 1 - i
        coefficients(cf, 0, af_ref, bf_ref)
        coefficients(cb, 1, ab_ref, bb_ref)

        def group(gi, carries):
            carry_f, carry_b = carries
            r0 = pl.multiple_of(gi * SUBLANES, SUBLANES)
            a, b = _scan8(af_ref[pl.ds(r0, SUBLANES), :], bf_ref[pl.ds(r0, SUBLANES), :], False)
            h = b + a * carry_f
            hf_ref[pl.ds(pl.multiple_of(cf * rows + r0, SUBLANES), SUBLANES), :] = h
            carry_f = h[SUBLANES - 1:SUBLANES]
            r0 = pl.multiple_of((n_groups - 1 - gi) * SUBLANES, SUBLANES)
            a, b = _scan8(ab_ref[pl.ds(r0, SUBLANES), :], bb_ref[pl.ds(r0, SUBLANES), :], True)
            h = b + a * carry_b
            hb_ref[pl.ds(pl.multiple_of(cb * rows + r0, SUBLANES), SUBLANES), :] = h
            return carry_f, h[0:1]

        return lax.fori_loop(0, n_groups, group, carries, unroll=LRU_UNROLL)

    carry_f, carry_b = lax.fori_loop(0, n_chunks, step, (initial(0), initial(1)))
    if emit_state:
        st_ref[0, 0] = carry_f
        st_ref[0, 1] = carry_b

    def finish(c, carry):
        sl = pl.ds(pl.multiple_of(c * rows, rows), rows)
        y_ref[0, sl, :] = ((hf_ref[sl, :] + hb_ref[sl, :]) * gate_ref[0, sl, :]).astype(BF16)
        return carry

    lax.fori_loop(0, n_chunks, finish, 0)


def _lru_core(main, wa, ba, wi, bi, lam, h0, *, batch, seq, emit_state):
    nb, bs = wa.shape[1], wa.shape[2]
    width = nb * bs
    rows = min(seq, 256)
    main = main.reshape(batch, seq, -1)
    vec = lambda p: p.astype(F32).reshape(2, 1, width)
    in_specs = [
        pl.BlockSpec((1, seq, bs), lambda b, n: (b, 0, n)),
        pl.BlockSpec((1, seq, bs), lambda b, n: (b, 0, nb + n)),
        pl.BlockSpec((2, 1, bs, bs), lambda b, n: (0, n, 0, 0)),
        pl.BlockSpec((2, 1, bs, bs), lambda b, n: (0, n, 0, 0)),
        pl.BlockSpec((2, 1, bs), lambda b, n: (0, 0, n)),
        pl.BlockSpec((2, 1, bs), lambda b, n: (0, 0, n)),
        pl.BlockSpec((2, 1, bs), lambda b, n: (0, 0, n)),
    ]
    args = [main, main, wa, wi, vec(ba), vec(bi), vec(lam)]
    state_spec = pl.BlockSpec((1, 2, 1, bs), lambda b, n: (b, 0, 0, n))
    if h0 is not None:
        in_specs.append(state_spec)
        args.append(h0.astype(F32).reshape(batch, 2, 1, width))
    out_specs = [pl.BlockSpec((1, seq, bs), lambda b, n: (b, 0, n))]
    out_shape = [jax.ShapeDtypeStruct((batch, seq, width), BF16)]
    if emit_state:
        out_specs.append(state_spec)
        out_shape.append(jax.ShapeDtypeStruct((batch, 2, 1, width), F32))
    outs = pl.pallas_call(
        functools.partial(_lru_kernel, seq=seq, rows=rows, has_h0=h0 is not None, emit_state=emit_state),
        grid=(batch, nb),
        in_specs=in_specs,
        out_specs=out_specs,
        out_shape=out_shape,
        scratch_shapes=[pltpu.VMEM((seq, bs), F32), pltpu.VMEM((seq, bs), F32)]
                       + [pltpu.VMEM((rows, bs), F32) for _ in range(4)],
        compiler_params=_cparams("parallel", "parallel"),
        name="lru_core",
    )(*args)
    return (outs[0], outs[1].reshape(batch, 2, width)) if emit_state else (outs[0], None)


def _to_col_major(h):
    b, l, d = h.shape
    return h.reshape(b, l // GRID_W, GRID_W, d).transpose(0, 2, 1, 3).reshape(b, l, d)


def _to_row_major(h):
    b, l, d = h.shape
    return h.reshape(b, GRID_W, l // GRID_W, d).transpose(0, 2, 1, 3).reshape(b, l, d)


def _conv_over(n_plain, conv_w, conv_b):
    cw = jnp.concatenate([jnp.zeros((conv_w.shape[0], n_plain), F32), conv_w.astype(F32)], axis=1)
    cb = jnp.concatenate([jnp.zeros((n_plain,), F32), conv_b.astype(F32)])
    return cw, cb


def _trunk(x, mod, grid, states, p, w):
    batch, seq, _ = x.shape
    depth = mod.shape[0]
    emit = states is None
    tn = 512
    finals = dict(ssd=[], ml_c=[], ml_n=[], ml_m=[], lru=[])
    for l in range(depth):
        sh1, sc1, g1, sh2, sc2, g2 = (mod[l, :, k][:, None, :] for k in range(6))
        kind, j = l % N_MIXERS, l // N_MIXERS
        col = grid and j % 2 == 1
        xin = _to_col_major(x) if col else x
        if kind == 0:
            inner = w["ssd_out"][j].shape[0]
            n_main = w["ssd_in"].shape[2] - LANES
            cw, cb = _conv_over(inner, p["ssd_conv_w"][j], p["ssd_conv_b"][j])
            kinds = ((inner // tn, "silu"), ((n_main - inner) // tn, "conv_silu"))
            main, small = _in_proj(xin, p["norm_mix_pre"][l], sc1, sh1, w["ssd_in"], w["ssd_in_dt"][j],
                                   layer=j, n=n_main, kinds=kinds, conv_w=cw, conv_b=cb, tn=tn)
            y, s = _ssd_core(main, small, p["ssd_dt_bias"][j], p["ssd_a_log"][j], p["ssd_d"][j], p["ssd_norm_w"][j],
                             None if emit else states["ssd"][:, j], batch=batch, seq=seq, emit_state=emit)
            finals["ssd"].append(s)
            w_out = w["ssd_out"][j]
        elif kind == 1:
            d_out = w["ml_out"][j].shape[0]
            n_main = w["ml_in"].shape[2] - w["ml_n_gates"]
            kinds = (((n_main - d_out) // tn, "none"), (d_out // tn, "sigmoid"))
            main, small = _in_proj(xin, p["norm_mix_pre"][l], sc1, sh1, w["ml_in"], w["ml_in_gates"][j],
                                   layer=j, n=n_main, kinds=kinds, tn=tn)
            st = None if emit else (states["ml_c"][:, j], states["ml_n"][:, j], states["ml_m"][:, j])
            y, s = _mlstm_core(main, small, p["ml_igate_b"][j], p["ml_fgate_b"][j], p["ml_norm_w"][j], st,
                               batch=batch, seq=seq, emit_state=emit)
            if emit:
                finals["ml_c"].append(s[0])
                finals["ml_n"].append(s[1])
                finals["ml_m"].append(s[2])
            w_out = w["ml_out"][j]
        else:
            width = w["lru_out"][j].shape[0]
            cw, cb = _conv_over(width, p["lru_conv_w"][j], p["lru_conv_b"][j])
            kinds = ((width // tn, "gelu"), (width // tn, "conv_none"))
            main, _ = _in_proj(xin, p["norm_mix_pre"][l], sc1, sh1, w["lru_in"], layer=j, n=2 * width, kinds=kinds,
                               conv_w=cw, conv_b=cb, tn=tn)
            y, s = _lru_core(main, w["lru_wa"][j], p["lru_ba"][j], w["lru_wi"][j], p["lru_bi"][j], p["lru_lambda"][j],
                             None if emit else states["lru"][:, j], batch=batch, seq=seq, emit_state=emit)
            finals["lru"].append(s)
            w_out = w["lru_out"][j]
        xo = _out_proj(y, w_out, xin, g1, p["norm_mix_post"][l])
        x = _to_row_major(xo) if col else xo
        x = _mlp(x, p["norm_mlp_pre"][l], sc2, sh2, w["mlp_w1"][l], w["mlp_w2"][l], g2, p["norm_mlp_post"][l])
    if emit:
        return x, {k: jnp.stack(v, axis=1) for k, v in finals.items()}
    return x, None


def kernel(x_prompt, x_sample, state_ssd, state_mlstm_C, state_mlstm_n, state_mlstm_m, state_rglru, c, c_ctx, mod_w, mod_b, norm_mix_pre, norm_mix_post, norm_mlp_pre, norm_mlp_post, mlp_w1, mlp_w2, ssd_in_w, ssd_conv_w, ssd_conv_b, ssd_dt_bias, ssd_a_log, ssd_d, ssd_norm_w, ssd_out_w, ml_in_w, ml_igate_b, ml_fgate_b, ml_norm_w, ml_out_w, lru_in_w, lru_conv_w, lru_conv_b, lru_wa, lru_ba, lru_wi, lru_bi, lru_lambda, lru_out_w):
    p = dict(norm_mix_pre=norm_mix_pre, norm_mix_post=norm_mix_post, norm_mlp_pre=norm_mlp_pre,
             norm_mlp_post=norm_mlp_post, ssd_conv_w=ssd_conv_w, ssd_conv_b=ssd_conv_b, ssd_dt_bias=ssd_dt_bias,
             ssd_a_log=ssd_a_log, ssd_d=ssd_d, ssd_norm_w=ssd_norm_w, ml_igate_b=ml_igate_b, ml_fgate_b=ml_fgate_b,
             ml_norm_w=ml_norm_w, lru_conv_w=lru_conv_w, lru_conv_b=lru_conv_b, lru_ba=lru_ba, lru_bi=lru_bi,
             lru_lambda=lru_lambda)
    d_model = x_prompt.shape[-1]
    depth = mod_w.shape[0]
    ssd_main = ssd_in_w.shape[2] - LANES
    ml_main = ml_in_w.shape[2] - 4 * state_mlstm_m.shape[-1]
    bf = lambda a: a.astype(BF16)
    n_ml, ml_heads = ml_in_w.shape[0], state_mlstm_m.shape[-1]
    ml_gates = ml_in_w[:, :, ml_main:].reshape(n_ml, d_model, 4, ml_heads).transpose(0, 3, 2, 1)
    ml_gates = jnp.pad(ml_gates, ((0, 0), (0, 0), (0, 4), (0, 0))).reshape(n_ml, 8 * ml_heads, d_model)
    ml_gates = jnp.pad(ml_gates, ((0, 0), (0, LANES - 8 * ml_heads), (0, 0)))
    w = dict(
        mlp_w1=bf(mlp_w1), mlp_w2=bf(mlp_w2),
        ssd_in=bf(ssd_in_w), ssd_in_dt=bf(ssd_in_w[:, :, ssd_main:].transpose(0, 2, 1)), ssd_out=bf(ssd_out_w),
        ml_in=bf(ml_in_w), ml_in_gates=bf(ml_gates), ml_n_gates=4 * ml_heads, ml_out=bf(ml_out_w),
        lru_in=bf(lru_in_w), lru_out=bf(lru_out_w), lru_wa=bf(lru_wa), lru_wi=bf(lru_wi),
    )
    n_dec = c.shape[0]
    n_rows = -(-(1 + n_dec) // SUBLANES) * SUBLANES
    cond = jnp.concatenate([c_ctx[None].astype(F32), c.astype(F32),
                            jnp.zeros((n_rows - 1 - n_dec, d_model), F32)], axis=0)
    mod = _mod_all(cond, mod_w, mod_b).reshape(depth, n_rows, 6, d_model)
    y_prompt, fin = _trunk(x_prompt, mod[:, 0:1], False, None, p, w)
    states = dict(ssd=state_ssd, ml_c=state_mlstm_C, ml_n=state_mlstm_n, ml_m=state_mlstm_m, lru=state_rglru)
    y_sample, _ = _trunk(x_sample, mod[:, 1:1 + n_dec], True, states, p, w)
    dt = x_prompt.dtype
    return (y_prompt, y_sample, fin["ssd"].astype(dt), fin["ml_c"].astype(dt), fin["ml_n"].astype(dt),
            fin["ml_m"].astype(dt), fin["lru"].astype(dt))
```

```python
import functools

import jax
import jax.numpy as jnp
from jax import lax
from jax.experimental import pallas as pl
from jax.experimental.pallas import tpu as pltpu

F32 = jnp.float32
BF16 = jnp.bfloat16
HIGHEST = lax.Precision.HIGHEST

EPS = 1e-6
CHUNK = 128
GRID_W = 64
N_MIXERS = 3
LRU_C = 8.0
SUBLANES = 8
LANES = 128
HALO = 16
CONV_ROWS = 1024
VMEM_LIMIT_BYTES = 56 * 1024 * 1024


def _cparams(*semantics):
    return pltpu.CompilerParams(dimension_semantics=semantics, vmem_limit_bytes=VMEM_LIMIT_BYTES)


def _dot(a, b):
    return jnp.dot(a.astype(BF16), b.astype(BF16), preferred_element_type=F32)


def _dot_exact(a, b):
    return jnp.dot(a, b, precision=HIGHEST, preferred_element_type=F32)


def _softplus(x):
    return jnp.maximum(x, 0.0) + jnp.log1p(jnp.exp(-jnp.abs(x)))


def _sigmoid(x):
    return jax.nn.sigmoid(x)


def _silu(x):
    return x * jax.nn.sigmoid(x)


def _rms(x, w):
    return x * lax.rsqrt(jnp.mean(x * x, axis=-1, keepdims=True) + EPS) * w


def _tri(n, lower):
    r = lax.broadcasted_iota(jnp.int32, (n, n), 0)
    c = lax.broadcasted_iota(jnp.int32, (n, n), 1)
    return (r >= c if lower else r <= c).astype(F32)


def _split3(v):
    hi = v.astype(BF16).astype(F32)
    r = v - hi
    mid = r.astype(BF16).astype(F32)
    return hi, mid, r - mid


_ACTS = {
    "none": lambda v: v,
    "silu": _silu,
    "sigmoid": _sigmoid,
    "gelu": lambda v: jax.nn.gelu(v, approximate=True),
}


def _mod_kernel(c_ref, w_ref, b_ref, o_ref):
    a = _silu(c_ref[...])
    o_ref[0] = _dot(a, w_ref[0]) + b_ref[0]


def _mod_all(cond, mod_w, mod_b):
    depth, d, n = mod_w.shape
    r = cond.shape[0]
    tn = 1024
    return pl.pallas_call(
        _mod_kernel,
        grid=(depth, n // tn),
        in_specs=[
            pl.BlockSpec((r, d), lambda l, j: (0, 0)),
            pl.BlockSpec((1, d, tn), lambda l, j: (l, 0, j)),
            pl.BlockSpec((1, 1, tn), lambda l, j: (l, 0, j)),
        ],
        out_specs=pl.BlockSpec((1, r, tn), lambda l, j: (l, 0, j)),
        out_shape=jax.ShapeDtypeStruct((depth, r, n), F32),
        compiler_params=_cparams("parallel", "parallel"),
        name="mod_map",
    )(cond, mod_w, mod_b.reshape(depth, 1, n))


def _in_proj_kernel(*refs, kinds, seq, tm, halo, has_conv, has_small):
    refs = list(refs)
    x_ref = refs.pop(0)
    xp_ref, xn_ref = (refs.pop(0), refs.pop(0)) if halo else (None, None)
    nw_ref, sc_ref, sh_ref, w_ref = refs.pop(0), refs.pop(0), refs.pop(0), refs.pop(0)
    cw_ref, cb_ref = (refs.pop(0), refs.pop(0)) if has_conv else (None, None)
    ws_ref = refs.pop(0) if has_small else None
    o_ref = refs.pop(0)
    os_ref = refs.pop(0) if has_small else None
    (h_ref,) = refs
    i, j = pl.program_id(0), pl.program_id(1)
    pad = HALO if halo else 0
    main = pl.ds(pad, tm)

    @pl.when(j == 0)
    def _():
        norm = lambda v: (_rms(v, nw_ref[...]) * (1.0 + sc_ref[0]) + sh_ref[0]).astype(BF16)
        h_ref[main, :] = norm(x_ref[...])
        if halo:
            h_ref[pl.ds(0, pad), :] = norm(xp_ref[...])
            h_ref[pl.ds(pad + tm, pad), :] = norm(xn_ref[...])
        if has_small:
            os_ref[...] = lax.dot_general(ws_ref[...], h_ref[main, :], (((1,), (1,)), ((), ())),
                                          preferred_element_type=F32)

    def plain(act):
        o_ref[...] = act(jnp.dot(h_ref[main, :], w_ref[0].astype(BF16), preferred_element_type=F32))

    def conv(act):
        w = cw_ref[...]
        wm = w_ref[0].astype(BF16)
        rb = min(CONV_ROWS, tm) if halo else min(tm, max(CONV_ROWS, seq))
        for blk in range(tm // rb):
            lo = blk * rb
            r = jnp.dot(h_ref[pl.ds(lo, rb + 2 * pad), :], wm, preferred_element_type=F32)
            n = r.shape[0]
            if halo:
                lead, tail = r[:pad], r[pad + rb:]
                if blk == 0:
                    lead = jnp.where((i * tm) % seq == 0, 0.0, lead)
                if blk == tm // rb - 1:
                    tail = jnp.where(((i + 1) * tm) % seq == 0, 0.0, tail)
                r = jnp.concatenate([lead, r[pad:pad + rb], tail], axis=0)
            tap = lambda shift: pltpu.roll(r, shift % n, 0)[pad:pad + rb]
            t2, t1, tn1 = tap(2), tap(1), tap(-1)
            if not halo:
                pos = (lo + lax.broadcasted_iota(jnp.int32, r.shape, 0)) & (seq - 1)
                t2 = jnp.where(pos >= 2, t2, 0.0)
                t1 = jnp.where(pos >= 1, t1, 0.0)
                tn1 = jnp.where(pos < seq - 1, tn1, 0.0)
            o_ref[pl.ds(lo, rb), :] = act(w[0:1] * t2 + w[1:2] * t1 + w[2:3] * r[pad:pad + rb] + w[3:4] * tn1
                                          + cb_ref[...])

    start = 0
    for count, kind in kinds:
        body = (functools.partial(conv, _ACTS[kind[5:]]) if kind.startswith("conv_")
                else functools.partial(plain, _ACTS[kind]))
        pl.when((j >= start) & (j < start + count))(body)
        start += count


def _in_proj(x, norm_w, scale, shift, w_all, w_small=None, *, layer, n, kinds, conv_w=None, conv_b=None,
             tm=1024, tn=512):
    b, l, d = x.shape
    m = b * l
    per_batch = scale.shape[0] > 1
    assert m % tm == 0 and n % tn == 0 and (not per_batch or l % tm == 0)
    assert sum(c for c, _ in kinds) == n // tn
    has_conv = any(k.startswith("conv_") for _, k in kinds)
    halo = has_conv and l > tm
    assert not has_conv or (l % tm == 0 if halo else (tm % l == 0 and l & (l - 1) == 0))
    mod_idx = (lambda i, j: ((i * tm) // l, 0, 0)) if per_batch else (lambda i, j: (0, 0, 0))
    has_small = w_small is not None
    x2 = x.reshape(m, d)
    in_specs = [pl.BlockSpec((tm, d), lambda i, j: (i, 0))]
    args = [x2]
    if halo:
        per_tile, last = tm // HALO, m // HALO - 1
        in_specs += [pl.BlockSpec((HALO, d), lambda i, j: (jnp.maximum(i * per_tile - 1, 0), 0)),
                     pl.BlockSpec((HALO, d), lambda i, j: (jnp.minimum((i + 1) * per_tile, last), 0))]
        args += [x2, x2]
    in_specs += [
        pl.BlockSpec((1, d), lambda i, j: (0, 0)),
        pl.BlockSpec((1, 1, d), mod_idx),
        pl.BlockSpec((1, 1, d), mod_idx),
        pl.BlockSpec((1, d, tn), lambda i, j: (layer, 0, j)),
    ]
    args += [norm_w.reshape(1, d), scale, shift, w_all]
    if has_conv:
        in_specs += [pl.BlockSpec((4, tn), lambda i, j: (0, j)), pl.BlockSpec((1, tn), lambda i, j: (0, j))]
        args += [conv_w.astype(F32), conv_b.astype(F32).reshape(1, n)]
    out_specs = [pl.BlockSpec((tm, tn), lambda i, j: (i, j))]
    out_shape = [jax.ShapeDtypeStruct((m, n), F32)]
    if has_small:
        ns = w_small.shape[0]
        in_specs.append(pl.BlockSpec((ns, d), lambda i, j: (0, 0)))
        out_specs.append(pl.BlockSpec((ns, tm), lambda i, j: (0, i)))
        out_shape.append(jax.ShapeDtypeStruct((ns, m), F32))
        args.append(w_small)
    outs = pl.pallas_call(
        functools.partial(_in_proj_kernel, kinds=tuple(kinds), seq=l, tm=tm, halo=halo, has_conv=has_conv,
                          has_small=has_small),
        grid=(m // tm, n // tn),
        in_specs=in_specs,
        out_specs=out_specs,
        out_shape=out_shape,
        scratch_shapes=[pltpu.VMEM((tm + (2 * HALO if halo else 0), d), BF16)],
        compiler_params=_cparams("parallel", "arbitrary"),
        name="in_proj",
    )(*args)
    return (outs[0], outs[1]) if has_small else (outs[0], None)


def _out_proj_kernel(y_ref, w_ref, x_ref, g_ref, nw_ref, o_ref, *acc, nk):
    k = pl.program_id(1)
    part = jnp.dot(y_ref[...], w_ref[...], preferred_element_type=F32)
    finish = lambda f: x_ref[...] + g_ref[0] * _rms(f, nw_ref[...])
    if nk == 1:
        o_ref[...] = finish(part)
        return
    (acc_ref,) = acc

    @pl.when(k == 0)
    def _():
        acc_ref[...] = part

    @pl.when((k > 0) & (k < nk - 1))
    def _():
        acc_ref[...] += part

    @pl.when(k == nk - 1)
    def _():
        o_ref[...] = finish(acc_ref[...] + part)


def _out_proj(y, w, x, gate, norm_w, *, tm=512, tk=2048):
    b, l, d = x.shape
    m = b * l
    kin = w.shape[0]
    per_batch = gate.shape[0] > 1
    assert m % tm == 0 and kin % tk == 0 and (not per_batch or l % tm == 0)
    mod_idx = (lambda i, k: ((i * tm) // l, 0, 0)) if per_batch else (lambda i, k: (0, 0, 0))
    nk = kin // tk
    out = pl.pallas_call(
        functools.partial(_out_proj_kernel, nk=nk),
        grid=(m // tm, nk),
        in_specs=[
            pl.BlockSpec((tm, tk), lambda i, k: (i, k)),
            pl.BlockSpec((tk, d), lambda i, k: (k, 0)),
            pl.BlockSpec((tm, d), lambda i, k: (i, 0)),
            pl.BlockSpec((1, 1, d), mod_idx),
            pl.BlockSpec((1, d), lambda i, k: (0, 0)),
        ],
        out_specs=pl.BlockSpec((tm, d), lambda i, k: (i, 0)),
        out_shape=jax.ShapeDtypeStruct((m, d), F32),
        scratch_shapes=[pltpu.VMEM((tm, d), F32)] if nk > 1 else [],
        compiler_params=_cparams("parallel", "arbitrary"),
        name="out_proj",
    )(y.reshape(m, kin), w, x.reshape(m, d), gate, norm_w.reshape(1, d))
    return out.reshape(b, l, d)


def _mlp_kernel(x_ref, nw_ref, sc_ref, sh_ref, w1_ref, w2_ref, g_ref, pw_ref, o_ref, h_ref, acc_ref, *, nf):
    j = pl.program_id(1)

    @pl.when(j == 0)
    def _():
        h = _rms(x_ref[...], nw_ref[...]) * (1.0 + sc_ref[0]) + sh_ref[0]
        h_ref[...] = h.astype(BF16)
        acc_ref[...] = jnp.zeros_like(acc_ref)

    t = jnp.dot(h_ref[...], w1_ref[...], preferred_element_type=F32)
    t = jnp.square(jnp.maximum(t, 0.0))
    acc_ref[...] += jnp.dot(t.astype(BF16), w2_ref[...], preferred_element_type=F32)

    @pl.when(j == nf - 1)
    def _():
        o_ref[...] = x_ref[...] + g_ref[0] * _rms(acc_ref[...], pw_ref[...])


def _mlp(x, pre_w, scale, shift, w1, w2, gate, post_w, *, tm=512, tf=1024):
    b, l, d = x.shape
    m = b * l
    dff = w1.shape[1]
    per_batch = scale.shape[0] > 1
    assert m % tm == 0 and dff % tf == 0 and (not per_batch or l % tm == 0)
    mod_idx = (lambda i, j: ((i * tm) // l, 0, 0)) if per_batch else (lambda i, j: (0, 0, 0))
    nf = dff // tf
    out = pl.pallas_call(
        functools.partial(_mlp_kernel, nf=nf),
        grid=(m // tm, nf),
        in_specs=[
            pl.BlockSpec((tm, d), lambda i, j: (i, 0)),
            pl.BlockSpec((1, d), lambda i, j: (0, 0)),
            pl.BlockSpec((1, 1, d), mod_idx),
            pl.BlockSpec((1, 1, d), mod_idx),
            pl.BlockSpec((d, tf), lambda i, j: (0, j)),
            pl.BlockSpec((tf, d), lambda i, j: (j, 0)),
            pl.BlockSpec((1, 1, d), mod_idx),
            pl.BlockSpec((1, d), lambda i, j: (0, 0)),
        ],
        out_specs=pl.BlockSpec((tm, d), lambda i, j: (i, 0)),
        out_shape=jax.ShapeDtypeStruct((m, d), F32),
        scratch_shapes=[pltpu.VMEM((tm, d), BF16), pltpu.VMEM((tm, d), F32)],
        compiler_params=_cparams("parallel", "arbitrary"),
        name="mlp",
    )(x.reshape(m, d), pre_w.reshape(1, d), scale, shift, w1, w2, gate, post_w.reshape(1, d))
    return out.reshape(b, l, d)


SSD_GROUP_HEADS = 8
SSD_HEAD_DIM = 64


def _ssd_kernel(*refs, seq, has_h0, emit_state):
    z_ref, x_ref, b_ref, c_ref, dtf_ref, dtb_ref, biast_ref, alogt_ref, dskip_ref, nw_ref = refs[:10]
    rest = list(refs[10:])
    h0_ref = rest.pop(0) if has_h0 else None
    y_ref = rest.pop(0)
    st_ref = rest.pop(0) if emit_state else None
    ya_ref, yt_ref, hf_ref, hb_ref, ct_ref, xt_ref, qry_ref, keyf_ref, keyb_ref, dts_ref, erow_ref, wrow_ref = rest
    n_chunks = seq // CHUNK
    gh, hd = SSD_GROUP_HEADS, SSD_HEAD_DIM
    hp = gh * hd
    wide = gh * CHUNK
    li = lax.broadcasted_iota(jnp.int32, (CHUNK, CHUNK), 0)
    si = lax.broadcasted_iota(jnp.int32, (CHUNK, CHUNK), 1)
    lane = lax.broadcasted_iota(jnp.int32, (CHUNK, LANES), 1)

    onehot = lambda cond: jnp.where(cond, 1.0, 0.0).astype(BF16)
    k_i = lax.broadcasted_iota(jnp.int32, (48, wide), 0)
    head_of_lane = lax.broadcasted_iota(jnp.int32, (48, wide), 1) >> 7
    pick_f = onehot((k_i & 15) == head_of_lane)
    pick_b = onehot((k_i & 15) == gh + head_of_lane)
    zero_tail = jnp.zeros((48, wide), BF16)
    diag_blocks = (lax.broadcasted_iota(jnp.int32, (gh, wide), 0)
                   == lax.broadcasted_iota(jnp.int32, (gh, wide), 1) >> 7)

    def per_head(t, rows):
        return jnp.concatenate([t[hd * h:hd * (h + 1)] * rows[h:h + 1] for h in range(gh)], axis=0)

    def key_rows(parts, d):
        blocks = [jnp.where(diag_blocks, -jnp.tile(p[d:d + gh], (1, gh)), 0.0) for p in parts]
        blocks.append(jnp.zeros((gh, wide), F32))
        return jnp.concatenate(blocks, axis=0).astype(BF16)

    dt_all = _softplus(jnp.concatenate([dtf_ref[...], dtb_ref[...]], axis=0) + biast_ref[0])
    dts_ref[...] = dt_all
    stack = lambda v: jnp.concatenate([v[:, CHUNK * c:CHUNK * (c + 1)] for c in range(n_chunks)], axis=0)
    dt_st = stack(dt_all)
    a_st = stack(dt_all * (-jnp.exp(alogt_ref[0])))
    cs = _dot_exact(a_st, _tri(CHUNK, False))
    tot = cs[:, CHUNK - 1:CHUNK]
    backward = (lax.broadcasted_iota(jnp.int32, cs.shape, 0) & (2 * gh - 1)) >= gh
    pos = jnp.where(backward, tot - cs + a_st, cs)
    erow_ref[...] = jnp.exp(pos)
    wrow_ref[...] = dt_st * jnp.exp(tot - pos)
    pos3 = _split3(pos)
    for c in range(n_chunks):
        blk, tok = slice(2 * gh * c, 2 * gh * (c + 1)), slice(CHUNK * c, CHUNK * (c + 1))
        parts = [p[blk] for p in pos3]
        qry_ref[tok, :] = jnp.concatenate([*parts, jnp.ones((32, CHUNK), F32), jnp.zeros((48, CHUNK), F32)],
                                          axis=0).T.astype(BF16)
        keyf_ref[32 * c:32 * (c + 1), :] = key_rows(parts, 0)
        keyb_ref[32 * c:32 * (c + 1), :] = key_rows(parts, gh)

    def within(c, carry):
        t0 = pl.multiple_of(c * CHUNK, CHUNK)
        rows = pl.ds(t0, CHUNK)
        x = x_ref[0, rows, :]
        cm = c_ref[0, rows, :]
        bt = b_ref[0, rows, :].T
        ct_ref[rows, :] = cm.T.astype(BF16)
        xt_ref[pl.ds(pl.multiple_of(c * hp, hp), hp), :] = x.T
        dtt = dts_ref[:, rows]
        query = qry_ref[rows, :]
        keys = pl.ds(pl.multiple_of(c * 32, 32), 32)
        arg_f = jnp.dot(query, jnp.concatenate([pick_f, keyf_ref[keys, :], zero_tail], axis=0),
                        preferred_element_type=F32)
        arg_b = jnp.dot(query, jnp.concatenate([pick_b, keyb_ref[keys, :], zero_tail], axis=0),
                        preferred_element_type=F32)
        cb = _dot(cm, bt)
        ydiag = []
        for p in range(gh // 2):
            wms = []
            for h in (2 * p, 2 * p + 1):
                arg = jnp.where(si <= li, arg_f[:, CHUNK * h:CHUNK * (h + 1)], arg_b[:, CHUNK * h:CHUNK * (h + 1)])
                coef = jnp.where(si < li, dtt[h:h + 1, :],
                                 jnp.where(si > li, dtt[gh + h:gh + h + 1, :], dtt[h:h + 1, :] + dtt[gh + h:gh + h + 1, :]))
                wms.append(cb * jnp.exp(arg) * coef)
            xp = x[:, LANES * p:LANES * (p + 1)]
            rhs = jnp.concatenate([jnp.where(lane < hd, xp, 0.0), jnp.where(lane >= hd, xp, 0.0)], axis=0)
            ydiag.append(_dot(jnp.concatenate(wms, axis=1), rhs))
        ya_ref[rows, :] = jnp.concatenate(ydiag, axis=1)
        return carry

    lax.fori_loop(0, n_chunks, within, 0, unroll=min(4, n_chunks))

    for d, st in ((0, hf_ref), (1, hb_ref)):
        st[...] = h0_ref[0, d].reshape(hp, -1) if has_h0 else jnp.zeros_like(st)
    yt_ref[...] = jnp.zeros_like(yt_ref)

    def carried(c, d, st, edge):
        rows = pl.ds(pl.multiple_of(c * CHUNK, CHUNK), CHUNK)
        chans = pl.ds(pl.multiple_of(c * hp, hp), hp)
        heads = pl.ds(pl.multiple_of(c * 2 * gh + d * gh, gh), gh)
        decay, weight = erow_ref[heads, :], wrow_ref[heads, :]
        h = st[...]
        yt_ref[chans, :] += per_head(_dot(h, ct_ref[rows, :]), decay)
        inc = _dot(per_head(xt_ref[chans, :], weight), b_ref[0, rows, :])
        st[...] = per_head(h, decay[:, edge:edge + 1]) + inc

    def across(i, carry):
        carried(i, 0, hf_ref, CHUNK - 1)
        carried(n_chunks - 1 - i, 1, hb_ref, 0)
        return carry

    lax.fori_loop(0, n_chunks, across, 0)
    if emit_state:
        for d, st in ((0, hf_ref), (1, hb_ref)):
            st_ref[0, d] = st[...].reshape(st_ref.shape[2:])

    def finish(c, carry):
        rows = pl.ds(pl.multiple_of(c * CHUNK, CHUNK), CHUNK)
        y = ya_ref[rows, :] + yt_ref[pl.ds(pl.multiple_of(c * hp, hp), hp), :].T
        y = (y + dskip_ref[...] * x_ref[0, rows, :]) * z_ref[0, rows, :]
        y_ref[0, rows, :] = _rms(y, nw_ref[...]).astype(BF16)
        return carry

    lax.fori_loop(0, n_chunks, finish, 0, unroll=2)


def _ssd_core(main, small, dt_bias, a_log, d_skip, norm_w, h0, *, batch, seq, emit_state):
    heads, groups, hd, hg = 64, 8, SSD_HEAD_DIM, SSD_GROUP_HEADS
    nstate = 128
    inner = heads * hd
    hp = inner // groups
    main = main.reshape(batch, seq, -1)

    def per_group(p):
        return p.astype(F32).reshape(2, groups, hg).transpose(1, 0, 2).reshape(groups, 2 * hg, 1)

    dsk = jnp.repeat(d_skip.astype(F32), hd).reshape(1, inner)
    nw = norm_w.astype(F32).reshape(1, inner)
    xo, bo, co = inner // hp, 2 * inner // nstate, (2 * inner + groups * nstate) // nstate
    in_specs = [
        pl.BlockSpec((1, seq, hp), lambda b, g: (b, 0, g)),
        pl.BlockSpec((1, seq, hp), lambda b, g: (b, 0, xo + g)),
        pl.BlockSpec((1, seq, nstate), lambda b, g: (b, 0, bo + g)),
        pl.BlockSpec((1, seq, nstate), lambda b, g: (b, 0, co + g)),
        pl.BlockSpec((hg, seq), lambda b, g: (g, b)),
        pl.BlockSpec((hg, seq), lambda b, g: (groups + g, b)),
        pl.BlockSpec((1, 2 * hg, 1), lambda b, g: (g, 0, 0)),
        pl.BlockSpec((1, 2 * hg, 1), lambda b, g: (g, 0, 0)),
        pl.BlockSpec((1, hp), lambda b, g: (0, g)),
        pl.BlockSpec((1, hp), lambda b, g: (0, g)),
    ]
    args = [main, main, main, main, small, small, per_group(dt_bias), per_group(a_log), dsk, nw]
    state_spec = pl.BlockSpec((1, 2, hg, hd, nstate), lambda b, g: (b, 0, g, 0, 0))
    if h0 is not None:
        in_specs.append(state_spec)
        args.append(h0.astype(F32))
    out_specs = [pl.BlockSpec((1, seq, hp), lambda b, g: (b, 0, g))]
    out_shape = [jax.ShapeDtypeStruct((batch, seq, inner), BF16)]
    if emit_state:
        out_specs.append(state_spec)
        out_shape.append(jax.ShapeDtypeStruct((batch, 2, heads, hd, nstate), F32))
    outs = pl.pallas_call(
        functools.partial(_ssd_kernel, seq=seq, has_h0=h0 is not None, emit_state=emit_state),
        grid=(batch, groups),
        in_specs=in_specs,
        out_specs=out_specs,
        out_shape=out_shape,
        scratch_shapes=[pltpu.VMEM((seq, hp), F32), pltpu.VMEM((seq // CHUNK * hp, CHUNK), F32),
                        pltpu.VMEM((hp, nstate), F32), pltpu.VMEM((hp, nstate), F32),
                        pltpu.VMEM((seq // CHUNK * nstate, CHUNK), BF16), pltpu.VMEM((seq // CHUNK * hp, CHUNK), F32),
                        pltpu.VMEM((seq, LANES), BF16),
                        pltpu.VMEM((seq // CHUNK * 32, hg * CHUNK), BF16), pltpu.VMEM((seq // CHUNK * 32, hg * CHUNK), BF16),
                        pltpu.VMEM((2 * hg, seq), F32),
                        pltpu.VMEM((seq // CHUNK * 2 * hg, CHUNK), F32), pltpu.VMEM((seq // CHUNK * 2 * hg, CHUNK), F32)],
        compiler_params=_cparams("parallel", "parallel"),
        name="ssd_core",
    )(*args)
    return (outs[0], outs[1]) if emit_state else (outs[0], None)


def _mlstm_kernel(*refs, seq, has_state, emit_state):
    q_ref, k_ref, v_ref, o_ref, gr_ref, br_ref, nw_ref = refs[:7]
    rest = list(refs[7:])
    if has_state:
        c0_ref, n0_ref, m0_ref = rest.pop(0), rest.pop(0), rest.pop(0)
    y_ref = rest.pop(0)
    if emit_state:
        cf_ref, nf_ref, mf_ref = rest.pop(0), rest.pop(0), rest.pop(0)
    sv_ref, rsum_ref, cmxb_ref, accb_ref, u_ref, rows_ref, cmx_ref, cols_ref, caf_ref, cab_ref = rest
    n_chunks = seq // CHUNK
    dqk = q_ref.shape[2]
    dv = v_ref.shape[2]
    li = lax.broadcasted_iota(jnp.int32, (CHUNK, CHUNK), 0)
    si = lax.broadcasted_iota(jnp.int32, (CHUNK, CHUNK), 1)
    ones_blk = jnp.ones((CHUNK, LANES), F32)
    k_scale = dqk ** -0.5

    def load_state(ca_ref, d):
        if has_state:
            ca_ref[...] = jnp.concatenate([c0_ref[0, d, 0], jnp.broadcast_to(n0_ref[0, d, 0], (dqk, LANES))], axis=1)
            return m0_ref[0, d, 0]
        ca_ref[...] = jnp.zeros_like(ca_ref)
        return jnp.zeros((1, 1), F32)

    def store_state(ca_ref, d, m):
        if emit_state:
            cf_ref[0, d, 0] = ca_ref[:, :dv]
            nf_ref[0, d, 0] = ca_ref[:, dv:dv + 1]
            mf_ref[0, d, 0] = m


    g_all = gr_ref[...] + br_ref[0]
    g_st = jnp.concatenate([g_all[:, CHUNK * c:CHUNK * (c + 1)] for c in range(n_chunks)], axis=0)
    r8 = lax.broadcasted_iota(jnp.int32, g_st.shape, 0) & 7
    tok = lax.broadcasted_iota(jnp.int32, g_st.shape, 1)
    g_st = jnp.where((r8 == 1) | (r8 == 3), -_softplus(-g_st), g_st)
    cum = _dot_exact(g_st, _tri(CHUNK, False))
    tot = cum[:, CHUNK - 1:CHUNK]
    acc = jnp.where(r8 == 1, cum, jnp.where(r8 == 3, tot - cum + g_st, 0.0))
    neg = g_st - pltpu.roll(acc, acc.shape[0] - 1, 0)
    pmax, smax = neg, neg
    for k in (1, 2, 4, 8, 16, 32, 64):
        pmax = jnp.where(tok >= k, jnp.maximum(pmax, pltpu.roll(pmax, k, 1)), pmax)
        smax = jnp.where(tok < CHUNK - k, jnp.maximum(smax, pltpu.roll(smax, CHUNK - k, 1)), smax)
    cmx = jnp.where(r8 == 0, pmax, jnp.where(r8 == 2, smax, 0.0))
    terms_all = jnp.where((r8 == 0) | (r8 == 2), neg, acc)
    rows_ref[...] = terms_all
    cmx_ref[...] = cmx
    for c in range(n_chunks):
        blk = slice(8 * c, 8 * (c + 1))
        cols_ref[CHUNK * c:CHUNK * (c + 1), :] = jnp.concatenate(
            [terms_all[blk], cmx[blk], jnp.zeros((CHUNK - 16, CHUNK), F32)], axis=0).T

    def within(c, carry):
        rows = pl.ds(pl.multiple_of(c * CHUNK, CHUNK), CHUNK)
        r8c = pl.ds(pl.multiple_of(c * 8, 8), 8)
        q = q_ref[0, rows, :]
        kt = (k_ref[0, rows, :] * k_scale).T
        v = v_ref[0, rows, :]
        vaug = jnp.concatenate([v, ones_blk], axis=1)
        qk = _dot(q, kt)
        terms, cmxr, cols = rows_ref[r8c, :], cmx_ref[r8c, :], cols_ref[rows, :]
        for d in (0, 1):
            neg_row = terms[2 * d:2 * d + 1]
            cmx_rep = jnp.broadcast_to(cols[:, 8 + 2 * d:9 + 2 * d], (CHUNK, LANES))
            cmxb_ref[d, rows, :] = cmx_rep
            accb_ref[d, rows, :] = jnp.broadcast_to(cols[:, 1 + 2 * d:2 + 2 * d], (CHUNK, LANES))
            s0 = jnp.where(si >= li if d else si <= li, qk * jnp.exp(neg_row - cmx_rep), 0.0)
            sv_ref[d, rows, :] = _dot(s0, v)
            rsum_ref[d, rows, :] = jnp.broadcast_to(jnp.sum(s0, axis=1, keepdims=True), (CHUNK, LANES))
            maxneg = cmxr[2:3, 0:1] if d else cmxr[0:1, CHUNK - 1:CHUNK]
            u_ref[d, pl.ds(pl.multiple_of(c * dqk, dqk), dqk), :] = _dot(kt * jnp.exp(neg_row - maxneg), vaug)
        return carry

    lax.fori_loop(0, n_chunks, within, 0, unroll=2)

    def carried(c, d, ca_ref, m):
        rows = pl.ds(pl.multiple_of(c * CHUNK, CHUNK), CHUNK)
        r8c = pl.ds(pl.multiple_of(c * 8, 8), 8)
        ca = ca_ref[...]
        qc = _dot(q_ref[0, rows, :], ca)
        cmx_rep = cmxb_ref[d, rows, :]
        big = jnp.maximum(m, cmx_rep)
        r = jnp.exp(cmx_rep - big)
        w_in = jnp.exp(m - big)
        den = r * rsum_ref[d, rows, :] + w_in * qc[:, dv:]
        inv = 1.0 / jnp.maximum(jnp.abs(den), jnp.exp(-(accb_ref[d, rows, :] + big)))
        wide = lambda t: jnp.concatenate([t] * (dv // LANES), axis=1)
        sv_ref[d, rows, :] = wide(r * inv) * sv_ref[d, rows, :] + wide(w_in * inv) * qc[:, :dv]
        terms, cmxr = rows_ref[r8c, :], cmx_ref[r8c, :]
        tot_c = terms[3:4, 0:1] if d else terms[1:2, CHUNK - 1:CHUNK]
        maxneg = cmxr[2:3, 0:1] if d else cmxr[0:1, CHUNK - 1:CHUNK]
        top = jnp.maximum(m, maxneg)
        ca_ref[...] = jnp.exp(m - top) * ca + jnp.exp(maxneg - top) * u_ref[d, pl.ds(pl.multiple_of(c * dqk, dqk), dqk), :]
        return tot_c + top

    def across(i, ms):
        return carried(i, 0, caf_ref, ms[0]), carried(n_chunks - 1 - i, 1, cab_ref, ms[1])

    m_f, m_b = lax.fori_loop(0, n_chunks, across, (load_state(caf_ref, 0), load_state(cab_ref, 1)))
    store_state(caf_ref, 0, m_f)
    store_state(cab_ref, 1, m_b)

    def finish(c, carry):
        rows = pl.ds(pl.multiple_of(c * CHUNK, CHUNK), CHUNK)
        h = _rms(sv_ref[0, rows, :] + sv_ref[1, rows, :], nw_ref[...])
        y_ref[0, rows, :] = (h * o_ref[0, rows, :]).astype(BF16)
        return carry

    lax.fori_loop(0, n_chunks, finish, 0)


def _mlstm_core(main, small, igate_b, fgate_b, norm_w, state, *, batch, seq, emit_state):
    heads, dqk, dv = 8, 128, 256
    main = main.reshape(batch, seq, -1)
    bias = jnp.stack([igate_b[0], fgate_b[0], igate_b[1], fgate_b[1]], axis=1).astype(F32)
    bias = jnp.pad(bias, ((0, 0), (0, 4)))
    in_specs = [
        pl.BlockSpec((1, seq, dqk), lambda b, h: (b, 0, h)),
        pl.BlockSpec((1, seq, dqk), lambda b, h: (b, 0, heads + h)),
        pl.BlockSpec((1, seq, dv), lambda b, h: (b, 0, heads + h)),
        pl.BlockSpec((1, seq, dv), lambda b, h: (b, 0, 2 * heads + h)),
        pl.BlockSpec((8, seq), lambda b, h: (h, b)),
        pl.BlockSpec((1, 8, 1), lambda b, h: (h, 0, 0)),
        pl.BlockSpec((1, dv), lambda b, h: (0, h)),
    ]
    args = [main, main, main, main, small, bias.reshape(heads, 8, 1), norm_w.astype(F32).reshape(1, heads * dv)]
    c_spec = pl.BlockSpec((1, 2, 1, dqk, dv), lambda b, h: (b, 0, h, 0, 0))
    n_spec = pl.BlockSpec((1, 2, 1, dqk, 1), lambda b, h: (b, 0, h, 0, 0))
    m_spec = pl.BlockSpec((1, 2, 1, 1, 1), lambda b, h: (b, 0, h, 0, 0))
    if state is not None:
        c0, n0, m0 = state
        in_specs += [c_spec, n_spec, m_spec]
        args += [c0.astype(F32), n0.astype(F32).reshape(batch, 2, heads, dqk, 1),
                 m0.astype(F32).reshape(batch, 2, heads, 1, 1)]
    out_specs = [pl.BlockSpec((1, seq, dv), lambda b, h: (b, 0, h))]
    out_shape = [jax.ShapeDtypeStruct((batch, seq, heads * dv), BF16)]
    if emit_state:
        out_specs += [c_spec, n_spec, m_spec]
        out_shape += [jax.ShapeDtypeStruct((batch, 2, heads, dqk, dv), F32),
                      jax.ShapeDtypeStruct((batch, 2, heads, dqk, 1), F32),
                      jax.ShapeDtypeStruct((batch, 2, heads, 1, 1), F32)]
    outs = pl.pallas_call(
        functools.partial(_mlstm_kernel, seq=seq, has_state=state is not None, emit_state=emit_state),
        grid=(batch, heads),
        in_specs=in_specs,
        out_specs=out_specs,
        out_shape=out_shape,
        scratch_shapes=[pltpu.VMEM((2, seq, dv), F32), pltpu.VMEM((2, seq, LANES), F32),
                        pltpu.VMEM((2, seq, LANES), F32), pltpu.VMEM((2, seq, LANES), F32),
                        pltpu.VMEM((2, seq // CHUNK * dqk, dv + LANES), F32),
                        pltpu.VMEM((seq // CHUNK * 8, CHUNK), F32), pltpu.VMEM((seq // CHUNK * 8, CHUNK), F32),
                        pltpu.VMEM((seq, LANES), F32),
                        pltpu.VMEM((dqk, dv + LANES), F32), pltpu.VMEM((dqk, dv + LANES), F32)],
        compiler_params=_cparams("parallel", "parallel"),
        name="mlstm_core",
    )(*args)
    if emit_state:
        return outs[0], (outs[1], outs[2].reshape(batch, 2, heads, dqk), outs[3].reshape(batch, 2, heads))
    return outs[0], None


LRU_UNROLL = 8


def _scan8(a, b, reverse):
    row = lax.broadcasted_iota(jnp.int32, a.shape, 0)
    for k in (1, 2, 4):
        shift = SUBLANES - k if reverse else k
        valid = row < SUBLANES - k if reverse else row >= k
        a_prev = pltpu.roll(a, shift, 0)
        b_prev = pltpu.roll(b, shift, 0)
        b = jnp.where(valid, a * b_prev, 0.0) + b
        a = jnp.where(valid, a * a_prev, a)
    return a, b


def _lru_kernel(*refs, seq, rows, has_h0, emit_state):
    gate_ref, x_ref, wa_ref, wi_ref, ba_ref, bi_ref, lam_ref = refs[:7]
    rest = list(refs[7:])
    h0_ref = rest.pop(0) if has_h0 else None
    y_ref = rest.pop(0)
    st_ref = rest.pop(0) if emit_state else None
    hf_ref, hb_ref, af_ref, bf_ref, ab_ref, bb_ref = rest
    n_chunks = seq // rows
    n_groups = rows // SUBLANES
    width = x_ref.shape[2]

    def coefficients(c, d, a_ref, b_ref):
        xc = x_ref[0, pl.ds(pl.multiple_of(c * rows, rows), rows), :]
        r = _sigmoid(_dot(xc, wa_ref[d, 0]) + ba_ref[d])
        i = _sigmoid(_dot(xc, wi_ref[d, 0]) + bi_ref[d])
        log_a = -LRU_C * r * _softplus(-lam_ref[d])
        a = jnp.exp(log_a)
        a_ref[...] = a
        b_ref[...] = jnp.sqrt(-jnp.tanh(log_a) * (1.0 + a * a)) * (i * xc)

    def initial(d):
        return h0_ref[0, d] if has_h0 else jnp.zeros((1, width), F32)

    def step(i, carries):
        cf, cb = i, n_chunks - 1 - i
        coefficients(cf, 0, af_ref, bf_ref)
        coefficients(cb, 1, ab_ref, bb_ref)

        def group(gi, carries):
            carry_f, carry_b = carries
            r0 = pl.multiple_of(gi * SUBLANES, SUBLANES)
            a, b = _scan8(af_ref[pl.ds(r0, SUBLANES), :], bf_ref[pl.ds(r0, SUBLANES), :], False)
            h = b + a * carry_f
            hf_ref[pl.ds(pl.multiple_of(cf * rows + r0, SUBLANES), SUBLANES), :] = h
            carry_f = h[SUBLANES - 1:SUBLANES]
            r0 = pl.multiple_of((n_groups - 1 - gi) * SUBLANES, SUBLANES)
            a, b = _scan8(ab_ref[pl.ds(r0, SUBLANES), :], bb_ref[pl.ds(r0, SUBLANES), :], True)
            h = b + a * carry_b
            hb_ref[pl.ds(pl.multiple_of(cb * rows + r0, SUBLANES), SUBLANES), :] = h
            return carry_f, h[0:1]

        return lax.fori_loop(0, n_groups, group, carries, unroll=LRU_UNROLL)

    carry_f, carry_b = lax.fori_loop(0, n_chunks, step, (initial(0), initial(1)))
    if emit_state:
        st_ref[0, 0] = carry_f
        st_ref[0, 1] = carry_b

    def finish(c, carry):
        sl = pl.ds(pl.multiple_of(c * rows, rows), rows)
        y_ref[0, sl, :] = ((hf_ref[sl, :] + hb_ref[sl, :]) * gate_ref[0, sl, :]).astype(BF16)
        return carry

    lax.fori_loop(0, n_chunks, finish, 0)


def _lru_core(main, wa, ba, wi, bi, lam, h0, *, batch, seq, emit_state):
    nb, bs = wa.shape[1], wa.shape[2]
    width = nb * bs
    rows = min(seq, 256)
    main = main.reshape(batch, seq, -1)
    vec = lambda p: p.astype(F32).reshape(2, 1, width)
    in_specs = [
        pl.BlockSpec((1, seq, bs), lambda b, n: (b, 0, n)),
        pl.BlockSpec((1, seq, bs), lambda b, n: (b, 0, nb + n)),
        pl.BlockSpec((2, 1, bs, bs), lambda b, n: (0, n, 0, 0)),
        pl.BlockSpec((2, 1, bs, bs), lambda b, n: (0, n, 0, 0)),
        pl.BlockSpec((2, 1, bs), lambda b, n: (0, 0, n)),
        pl.BlockSpec((2, 1, bs), lambda b, n: (0, 0, n)),
        pl.BlockSpec((2, 1, bs), lambda b, n: (0, 0, n)),
    ]
    args = [main, main, wa, wi, vec(ba), vec(bi), vec(lam)]
    state_spec = pl.BlockSpec((1, 2, 1, bs), lambda b, n: (b, 0, 0, n))
    if h0 is not None:
        in_specs.append(state_spec)
        args.append(h0.astype(F32).reshape(batch, 2, 1, width))
    out_specs = [pl.BlockSpec((1, seq, bs), lambda b, n: (b, 0, n))]
    out_shape = [jax.ShapeDtypeStruct((batch, seq, width), BF16)]
    if emit_state:
        out_specs.append(state_spec)
        out_shape.append(jax.ShapeDtypeStruct((batch, 2, 1, width), F32))
    outs = pl.pallas_call(
        functools.partial(_lru_kernel, seq=seq, rows=rows, has_h0=h0 is not None, emit_state=emit_state),
        grid=(batch, nb),
        in_specs=in_specs,
        out_specs=out_specs,
        out_shape=out_shape,
        scratch_shapes=[pltpu.VMEM((seq, bs), F32), pltpu.VMEM((seq, bs), F32)]
                       + [pltpu.VMEM((rows, bs), F32) for _ in range(4)],
        compiler_params=_cparams("parallel", "parallel"),
        name="lru_core",
    )(*args)
    return (outs[0], outs[1].reshape(batch, 2, width)) if emit_state else (outs[0], None)


def _to_col_major(h):
    b, l, d = h.shape
    return h.reshape(b, l // GRID_W, GRID_W, d).transpose(0, 2, 1, 3).reshape(b, l, d)


def _to_row_major(h):
    b, l, d = h.shape
    return h.reshape(b, GRID_W, l // GRID_W, d).transpose(0, 2, 1, 3).reshape(b, l, d)


def _conv_over(n_plain, conv_w, conv_b):
    cw = jnp.concatenate([jnp.zeros((conv_w.shape[0], n_plain), F32), conv_w.astype(F32)], axis=1)
    cb = jnp.concatenate([jnp.zeros((n_plain,), F32), conv_b.astype(F32)])
    return cw, cb


def _trunk(x, mod, grid, states, p, w):
    batch, seq, _ = x.shape
    depth = mod.shape[0]
    emit = states is None
    tn = 512
    finals = dict(ssd=[], ml_c=[], ml_n=[], ml_m=[], lru=[])
    for l in range(depth):
        sh1, sc1, g1, sh2, sc2, g2 = (mod[l, :, k][:, None, :] for k in range(6))
        kind, j = l % N_MIXERS, l // N_MIXERS
        col = grid and j % 2 == 1
        xin = _to_col_major(x) if col else x
        if kind == 0:
            inner = w["ssd_out"][j].shape[0]
            n_main = w["ssd_in"].shape[2] - LANES
            cw, cb = _conv_over(inner, p["ssd_conv_w"][j], p["ssd_conv_b"][j])
            kinds = ((inner // tn, "silu"), ((n_main - inner) // tn, "conv_silu"))
            main, small = _in_proj(xin, p["norm_mix_pre"][l], sc1, sh1, w["ssd_in"], w["ssd_in_dt"][j],
                                   layer=j, n=n_main, kinds=kinds, conv_w=cw, conv_b=cb, tn=tn)
            y, s = _ssd_core(main, small, p["ssd_dt_bias"][j], p["ssd_a_log"][j], p["ssd_d"][j], p["ssd_norm_w"][j],
                             None if emit else states["ssd"][:, j], batch=batch, seq=seq, emit_state=emit)
            finals["ssd"].append(s)
            w_out = w["ssd_out"][j]
        elif kind == 1:
            d_out = w["ml_out"][j].shape[0]
            n_main = w["ml_in"].shape[2] - w["ml_n_gates"]
            kinds = (((n_main - d_out) // tn, "none"), (d_out // tn, "sigmoid"))
            main, small = _in_proj(xin, p["norm_mix_pre"][l], sc1, sh1, w["ml_in"], w["ml_in_gates"][j],
                                   layer=j, n=n_main, kinds=kinds, tn=tn)
            st = None if emit else (states["ml_c"][:, j], states["ml_n"][:, j], states["ml_m"][:, j])
            y, s = _mlstm_core(main, small, p["ml_igate_b"][j], p["ml_fgate_b"][j], p["ml_norm_w"][j], st,
                               batch=batch, seq=seq, emit_state=emit)
            if emit:
                finals["ml_c"].append(s[0])
                finals["ml_n"].append(s[1])
                finals["ml_m"].append(s[2])
            w_out = w["ml_out"][j]
        else:
            width = w["lru_out"][j].shape[0]
            cw, cb = _conv_over(width, p["lru_conv_w"][j], p["lru_conv_b"][j])
            kinds = ((width // tn, "gelu"), (width // tn, "conv_none"))
            main, _ = _in_proj(xin, p["norm_mix_pre"][l], sc1, sh1, w["lru_in"], layer=j, n=2 * width, kinds=kinds,
                               conv_w=cw, conv_b=cb, tn=tn)
            y, s = _lru_core(main, w["lru_wa"][j], p["lru_ba"][j], w["lru_wi"][j], p["lru_bi"][j], p["lru_lambda"][j],
                             None if emit else states["lru"][:, j], batch=batch, seq=seq, emit_state=emit)
            finals["lru"].append(s)
            w_out = w["lru_out"][j]
        xo = _out_proj(y, w_out, xin, g1, p["norm_mix_post"][l])
        x = _to_row_major(xo) if col else xo
        x = _mlp(x, p["norm_mlp_pre"][l], sc2, sh2, w["mlp_w1"][l], w["mlp_w2"][l], g2, p["norm_mlp_post"][l])
    if emit:
        return x, {k: jnp.stack(v, axis=1) for k, v in finals.items()}
    return x, None


def kernel(x_prompt, x_sample, state_ssd, state_mlstm_C, state_mlstm_n, state_mlstm_m, state_rglru, c, c_ctx, mod_w, mod_b, norm_mix_pre, norm_mix_post, norm_mlp_pre, norm_mlp_post, mlp_w1, mlp_w2, ssd_in_w, ssd_conv_w, ssd_conv_b, ssd_dt_bias, ssd_a_log, ssd_d, ssd_norm_w, ssd_out_w, ml_in_w, ml_igate_b, ml_fgate_b, ml_norm_w, ml_out_w, lru_in_w, lru_conv_w, lru_conv_b, lru_wa, lru_ba, lru_wi, lru_bi, lru_lambda, lru_out_w):
    p = dict(norm_mix_pre=norm_mix_pre, norm_mix_post=norm_mix_post, norm_mlp_pre=norm_mlp_pre,
             norm_mlp_post=norm_mlp_post, ssd_conv_w=ssd_conv_w, ssd_conv_b=ssd_conv_b, ssd_dt_bias=ssd_dt_bias,
             ssd_a_log=ssd_a_log, ssd_d=ssd_d, ssd_norm_w=ssd_norm_w, ml_igate_b=ml_igate_b, ml_fgate_b=ml_fgate_b,
             ml_norm_w=ml_norm_w, lru_conv_w=lru_conv_w, lru_conv_b=lru_conv_b, lru_ba=lru_ba, lru_bi=lru_bi,
             lru_lambda=lru_lambda)
    d_model = x_prompt.shape[-1]
    depth = mod_w.shape[0]
    ssd_main = ssd_in_w.shape[2] - LANES
    ml_main = ml_in_w.shape[2] - 4 * state_mlstm_m.shape[-1]
    bf = lambda a: a.astype(BF16)
    n_ml, ml_heads = ml_in_w.shape[0], state_mlstm_m.shape[-1]
    ml_gates = ml_in_w[:, :, ml_main:].reshape(n_ml, d_model, 4, ml_heads).transpose(0, 3, 2, 1)
    ml_gates = jnp.pad(ml_gates, ((0, 0), (0, 0), (0, 4), (0, 0))).reshape(n_ml, 8 * ml_heads, d_model)
    ml_gates = jnp.pad(ml_gates, ((0, 0), (0, LANES - 8 * ml_heads), (0, 0)))
    w = dict(
        mlp_w1=bf(mlp_w1), mlp_w2=bf(mlp_w2),
        ssd_in=bf(ssd_in_w), ssd_in_dt=bf(ssd_in_w[:, :, ssd_main:].transpose(0, 2, 1)), ssd_out=bf(ssd_out_w),
        ml_in=bf(ml_in_w), ml_in_gates=bf(ml_gates), ml_n_gates=4 * ml_heads, ml_out=bf(ml_out_w),
        lru_in=bf(lru_in_w), lru_out=bf(lru_out_w), lru_wa=bf(lru_wa), lru_wi=bf(lru_wi),
    )
    n_dec = c.shape[0]
    n_rows = -(-(1 + n_dec) // SUBLANES) * SUBLANES
    cond = jnp.concatenate([c_ctx[None].astype(F32), c.astype(F32),
                            jnp.zeros((n_rows - 1 - n_dec, d_model), F32)], axis=0)
    mod = _mod_all(cond, mod_w, mod_b).reshape(depth, n_rows, 6, d_model)
    y_prompt, fin = _trunk(x_prompt, mod[:, 0:1], False, None, p, w)
    states = dict(ssd=state_ssd, ml_c=state_mlstm_C, ml_n=state_mlstm_n, ml_m=state_mlstm_m, lru=state_rglru)
    y_sample, _ = _trunk(x_sample, mod[:, 1:1 + n_dec], True, states, p, w)
    dt = x_prompt.dtype
    return (y_prompt, y_sample, fin["ssd"].astype(dt), fin["ml_c"].astype(dt), fin["ml_n"].astype(dt),
            fin["ml_m"].astype(dt), fin["lru"].astype(dt))
```
